```python
import math
import jax, jax.numpy as jnp
from jax import lax
import numpy as np

D_MODEL = 1024
BATCH = 16
SEQ = 2048
DEPTH = 2
DEC_BATCH = 32
DEC_SEQ = 4
PAST_LEN = 16384
PAGE_SIZE = 128

N_A_LAYERS = DEPTH // 2
N_B_LAYERS = DEPTH - N_A_LAYERS

GDN_HEADS = 8
GDN_DK = 128
GDN_DV = 128
CONV_W = 4
GDN_CHUNK = 64
QKV_DIM = GDN_HEADS * (2 * GDN_DK + GDN_DV)
Z_DIM = GDN_HEADS * GDN_DV
IN_A_DIM = QKV_DIM + Z_DIM + 2 * GDN_HEADS

MOBA_HEADS = 8
MOBA_HD = 128
MOBA_BLOCK = 256
MOBA_TOPK = 3
Q_BLOCK = 128
ATTN_DIM = MOBA_HEADS * MOBA_HD

N_GROUPS = 4
EXPERTS_PER_GROUP = 8
N_EXPERTS = N_GROUPS * EXPERTS_PER_GROUP
EXPERT_FF = 256
TOPK_IN_GROUP = 2

RMS_EPS = 1e-6
L2_EPS = 1e-6

kernel_name = "yoco_gdn_moba_hmoe_step"


def rmsnorm(x, g):
    xf = x.astype(jnp.float32)
    y = xf * lax.rsqrt(jnp.mean(xf * xf, axis=-1, keepdims=True) + RMS_EPS)
    return (y * g.astype(jnp.float32)).astype(x.dtype)


def l2norm(x):
    return x * lax.rsqrt(jnp.sum(x * x, axis=-1, keepdims=True) + L2_EPS)


def gated_delta_rule(q, k, v, beta, g, s0):
    bsz, t, nh, _ = q.shape
    c = min(GDN_CHUNK, t)
    n = -(-t // c)
    pad = n * c - t

    def prep(a):
        a = jnp.pad(a, [(0, 0), (0, pad)] + [(0, 0)] * (a.ndim - 2))
        a = a.reshape((bsz, n, c) + a.shape[2:])
        return a.transpose((1, 0, 3, 2) + tuple(range(4, a.ndim)))

    xs = (prep(q), prep(k), prep(v), prep(beta), prep(g))
    incl = jnp.tril(jnp.ones((c, c), dtype=bool))
    strict = jnp.tril(jnp.ones((c, c), dtype=bool), -1)

    def step(s, inp):
        qc, kc, vc, bc, gc = inp
        gcum = jnp.cumsum(gc, axis=-1)
        decay = jnp.exp(jnp.where(incl, gcum[..., :, None] - gcum[..., None, :], -jnp.inf))
        kk = jnp.einsum('bhtd,bhjd->bhtj', kc, kc)
        m = jnp.where(strict, bc[..., :, None] * decay * kk, 0.0)
        a_mat = m + jnp.eye(c, dtype=m.dtype)
        rhs = bc[..., None] * (vc - jnp.exp(gcum)[..., None] * jnp.einsum('bhtd,bhdv->bhtv', kc, s))
        u = lax.linalg.triangular_solve(a_mat, rhs, left_side=True, lower=True, unit_diagonal=True)
        qk = jnp.einsum('bhtd,bhjd->bhtj', qc, kc) * decay
        o = jnp.exp(gcum)[..., None] * jnp.einsum('bhtd,bhdv->bhtv', qc, s) + jnp.einsum('bhtj,bhjv->bhtv', qk, u)
        glast = gcum[..., -1:]
        s_new = jnp.exp(glast)[..., None] * s + jnp.einsum('bhjd,bhjv->bhdv', kc * jnp.exp(glast - gcum)[..., None], u)
        return s_new, o

    s_fin, o = lax.scan(step, s0, xs)
    o = o.transpose(1, 0, 3, 2, 4).reshape(bsz, n * c, nh, o.shape[-1])[:, :t]
    return o, s_fin


def gdn_mixer(h, conv_prev, s0, w_in, conv_w, a_log, dt_bias, norm_g, w_out):
    bsz, t, _ = h.shape
    proj = h @ w_in
    qkv_raw = proj[..., :QKV_DIM]
    z = proj[..., QKV_DIM:QKV_DIM + Z_DIM]
    b_raw = proj[..., QKV_DIM + Z_DIM:QKV_DIM + Z_DIM + GDN_HEADS]
    a_raw = proj[..., QKV_DIM + Z_DIM + GDN_HEADS:]
    xpad = jnp.concatenate([conv_prev.astype(qkv_raw.dtype), qkv_raw], axis=1)
    conv = sum(xpad[:, i:i + t] * conv_w[i] for i in range(CONV_W))
    new_conv = xpad[:, t:]
    qkv = jax.nn.silu(conv.astype(jnp.float32))
    hk = GDN_HEADS * GDN_DK
    q = l2norm(qkv[..., :hk].reshape(bsz, t, GDN_HEADS, GDN_DK)) * (GDN_DK ** -0.5)
    k = l2norm(qkv[..., hk:2 * hk].reshape(bsz, t, GDN_HEADS, GDN_DK))
    v = qkv[..., 2 * hk:].reshape(bsz, t, GDN_HEADS, GDN_DV)
    beta = jax.nn.sigmoid(b_raw.astype(jnp.float32))
    g = -jnp.exp(a_log.astype(jnp.float32)) * jax.nn.softplus(a_raw.astype(jnp.float32) + dt_bias.astype(jnp.float32))
    o, s_new = gated_delta_rule(q, k, v, beta, g, s0.astype(jnp.float32))
    zg = jax.nn.silu(z.astype(jnp.float32)).reshape(bsz, t, GDN_HEADS, GDN_DV)
    o = o * lax.rsqrt(jnp.mean(o * o, axis=-1, keepdims=True) + RMS_EPS) * norm_g.astype(jnp.float32) * zg
    out = o.reshape(bsz, t, Z_DIM).astype(h.dtype) @ w_out
    return out, new_conv, s_new


def moba_context_prompt(k, v):
    bsz, s = k.shape[:2]
    nbp = -(-s // MOBA_BLOCK)
    pad = nbp * MOBA_BLOCK - s
    kb = jnp.pad(k, ((0, 0), (0, pad), (0, 0), (0, 0))).reshape(bsz, nbp, MOBA_BLOCK, MOBA_HEADS, MOBA_HD)
    vb = jnp.pad(v, ((0, 0), (0, pad), (0, 0), (0, 0))).reshape(bsz, nbp, MOBA_BLOCK, MOBA_HEADS, MOBA_HD)
    kmean = jnp.mean(kb, axis=2, dtype=jnp.float32)
    return kmean, kb.transpose(0, 3, 1, 2, 4), vb.transpose(0, 3, 1, 2, 4)


def moba_attend_prompt(q, kmean, kbh, vbh):
    bsz, s = q.shape[:2]
    nbp = kbh.shape[2]
    scale = MOBA_HD ** -0.5
    n_sel = min(MOBA_TOPK, (s - 1) // MOBA_BLOCK)
    nq = s // Q_BLOCK
    qc = q.reshape(bsz * nq, Q_BLOCK, MOBA_HEADS, MOBA_HD)
    b_ids = jnp.arange(bsz * nq, dtype=jnp.int32) // nq
    c_ids = jnp.arange(bsz * nq, dtype=jnp.int32) % nq
    if n_sel > 0:
        own = jnp.arange(s) // MOBA_BLOCK
        past = jnp.arange(nbp)[None, :] < own[:, None]
        gate = jnp.einsum('bshd,bnhd->bshn', q.astype(jnp.float32), kmean)
        gate = jnp.where(past[None, :, None, :], gate, -jnp.inf)
        vals, idx = lax.top_k(gate, n_sel)
        valid = jnp.isfinite(vals)
        xs = (qc, b_ids, c_ids, idx.reshape(bsz * nq, Q_BLOCK, MOBA_HEADS, n_sel),
              valid.reshape(bsz * nq, Q_BLOCK, MOBA_HEADS, n_sel))
    else:
        xs = (qc, b_ids, c_ids)
    heads = jnp.arange(MOBA_HEADS)[None, :, None]

    def step(inp):
        qb, b, c = inp[0], inp[1], inp[2]
        kb_b = kbh[b]
        vb_b = vbh[b]
        ob = (c * Q_BLOCK) // MOBA_BLOCK
        k_own = lax.dynamic_index_in_dim(kb_b, ob, axis=1, keepdims=False)
        v_own = lax.dynamic_index_in_dim(vb_b, ob, axis=1, keepdims=False)
        pos_q = c * Q_BLOCK + jnp.arange(Q_BLOCK)
        pos_k = ob * MOBA_BLOCK + jnp.arange(MOBA_BLOCK)
        l_own = jnp.einsum('qhd,hkd->qhk', qb, k_own).astype(jnp.float32) * scale
        l_own = jnp.where(pos_k[None, None, :] <= pos_q[:, None, None], l_own, -jnp.inf)
        if n_sel > 0:
            idx_b, valid_b = inp[3], inp[4]
            k_sel = kb_b[heads, idx_b].reshape(Q_BLOCK, MOBA_HEADS, n_sel * MOBA_BLOCK, MOBA_HD)
            v_sel = vb_b[heads, idx_b].reshape(Q_BLOCK, MOBA_HEADS, n_sel * MOBA_BLOCK, MOBA_HD)
            l_sel = jnp.einsum('qhd,qhkd->qhk', qb, k_sel).astype(jnp.float32) * scale
            l_sel = jnp.where(jnp.repeat(valid_b, MOBA_BLOCK, axis=-1), l_sel, -jnp.inf)
            p = jax.nn.softmax(jnp.concatenate([l_sel, l_own], axis=-1), axis=-1).astype(vb_b.dtype)
            o = (jnp.einsum('qhk,qhkd->qhd', p[..., :n_sel * MOBA_BLOCK], v_sel)
                 + jnp.einsum('qhk,hkd->qhd', p[..., n_sel * MOBA_BLOCK:], v_own))
        else:
            p = jax.nn.softmax(l_own, axis=-1).astype(vb_b.dtype)
            o = jnp.einsum('qhk,hkd->qhd', p, v_own)
        return o

    o = lax.map(step, xs)
    return o.reshape(bsz, s, ATTN_DIM)


def moba_context_sample(k_new, v_new, cache_k, cache_v, page_table):
    dbsz = k_new.shape[0]
    n_pages = PAST_LEN // PAGE_SIZE
    ppb = MOBA_BLOCK // PAGE_SIZE
    n_full = PAST_LEN // MOBA_BLOCK
    tail_len = (n_pages - n_full * ppb) * PAGE_SIZE
    tail_pt = page_table[:, n_full * ppb:]
    k_tail = cache_k[tail_pt].reshape(dbsz, tail_len, MOBA_HEADS, MOBA_HD).astype(k_new.dtype)
    v_tail = cache_v[tail_pt].reshape(dbsz, tail_len, MOBA_HEADS, MOBA_HD).astype(v_new.dtype)
    k_own = jnp.concatenate([k_tail, k_new], axis=1)
    v_own = jnp.concatenate([v_tail, v_new], axis=1)
    if n_full > 0:
        page_sums = jnp.sum(cache_k, axis=1, dtype=jnp.float32)
        bmean = page_sums[page_table[:, :n_full * ppb]].reshape(dbsz, n_full, ppb, MOBA_HEADS, MOBA_HD)
        bmean = jnp.sum(bmean, axis=2) / MOBA_BLOCK
    else:
        bmean = None
    return bmean, k_own, v_own


def moba_attend_sample(q, bmean, k_own, v_own, cache_k, cache_v, page_table):
    dbsz, t = q.shape[:2]
    ppb = MOBA_BLOCK // PAGE_SIZE
    n_full = PAST_LEN // MOBA_BLOCK
    tail_len = k_own.shape[1] - t
    n_sel = min(MOBA_TOPK, n_full)
    scale = MOBA_HD ** -0.5
    l_own = jnp.einsum('bthd,blhd->bthl', q, k_own).astype(jnp.float32) * scale
    key_rel = jnp.arange(tail_len + t) - tail_len
    own_mask = key_rel[None, :] <= jnp.arange(t)[:, None]
    l_own = jnp.where(own_mask[None, :, None, :], l_own, -jnp.inf)
    if n_sel > 0:
        gate = jnp.einsum('bthd,bnhd->bthn', q.astype(jnp.float32), bmean)
        _, idx = lax.top_k(gate, n_sel)
        logical = idx[..., None] * ppb + jnp.arange(ppb)
        phys = page_table[jnp.arange(dbsz)[:, None, None, None, None], logical]
        heads = jnp.arange(MOBA_HEADS)[None, None, :, None, None]
        k_sel = cache_k[phys, :, heads, :].reshape(dbsz, t, MOBA_HEADS, n_sel * MOBA_BLOCK, MOBA_HD).astype(q.dtype)
        v_sel = cache_v[phys, :, heads, :].reshape(dbsz, t, MOBA_HEADS, n_sel * MOBA_BLOCK, MOBA_HD).astype(v_own.dtype)
        l_sel = jnp.einsum('bthd,bthkd->bthk', q, k_sel).astype(jnp.float32) * scale
        p = jax.nn.softmax(jnp.concatenate([l_sel, l_own], axis=-1), axis=-1).astype(v_own.dtype)
        o = (jnp.einsum('bthk,bthkd->bthd', p[..., :n_sel * MOBA_BLOCK], v_sel)
             + jnp.einsum('bthl,blhd->bthd', p[..., n_sel * MOBA_BLOCK:], v_own))
    else:
        p = jax.nn.softmax(l_own, axis=-1).astype(v_own.dtype)
        o = jnp.einsum('bthl,blhd->bthd', p, v_own)
    return o.reshape(dbsz, t, ATTN_DIM)


def hier_moe(h, w_group, b_group, w_router, b_router, w_gate, w_up, w_down):
    shp = h.shape
    x = h.reshape(-1, D_MODEL)
    n = x.shape[0]
    pg = jax.nn.softmax((x @ w_group).astype(jnp.float32) + b_group.astype(jnp.float32), axis=-1)
    pg_sel, g_idx = lax.top_k(pg, 1)
    le = ((x @ w_router).astype(jnp.float32) + b_router.astype(jnp.float32)).reshape(n, N_GROUPS, EXPERTS_PER_GROUP)
    le = jnp.take_along_axis(le, g_idx[:, :, None], axis=1)[:, 0]
    pe = jax.nn.softmax(le, axis=-1)
    top_v, top_i = lax.top_k(pe, TOPK_IN_GROUP)
    w = pg_sel * top_v / jnp.sum(top_v, axis=-1, keepdims=True)
    e_idx = g_idx * EXPERTS_PER_GROUP + top_i
    combine = jnp.einsum('nk,nke->ne', w, jax.nn.one_hot(e_idx, N_EXPERTS, dtype=jnp.float32))
    out = jnp.zeros((n, D_MODEL), jnp.float32)
    for e in range(N_EXPERTS):
        he = jax.nn.silu(x @ w_gate[e]) * (x @ w_up[e])
        out = out + combine[:, e:e + 1] * (he @ w_down[e]).astype(jnp.float32)
    return out.astype(h.dtype).reshape(shp)


def setup_inputs(seed: int = 0) -> dict:
    key = jax.random.key(seed)
    ks = jax.random.split(key, 32)
    f32 = jnp.float32
    n_pages = PAST_LEN // PAGE_SIZE
    n_used = DEC_BATCH * n_pages
    n_pool = n_used + max(1, n_used // 4)

    def nrm(k, shape, scale):
        return jax.random.normal(k, shape, f32) * scale

    dt = jnp.exp(jax.random.uniform(ks[12], (N_A_LAYERS, GDN_HEADS), f32, math.log(1e-3), math.log(1e-1)))
    return {
        "x_prompt": nrm(ks[0], (BATCH, SEQ, D_MODEL), 1.0),
        "x_sample": nrm(ks[1], (DEC_BATCH, DEC_SEQ, D_MODEL), 1.0),
        "state_gdn": nrm(ks[2], (N_A_LAYERS, DEC_BATCH, GDN_HEADS, GDN_DK, GDN_DV), 0.1),
        "state_conv": nrm(ks[3], (N_A_LAYERS, DEC_BATCH, CONV_W - 1, QKV_DIM), 1.0),
        "cache_k": nrm(ks[4], (n_pool, PAGE_SIZE, MOBA_HEADS, MOBA_HD), 1.0),
        "cache_v": nrm(ks[5], (n_pool, PAGE_SIZE, MOBA_HEADS, MOBA_HD), 1.0),
        "page_table": jax.random.permutation(ks[6], n_pool)[:n_used].reshape(DEC_BATCH, n_pages).astype(jnp.int32),
        "norm_mix": 1.0 + nrm(ks[7], (DEPTH, D_MODEL), 0.02),
        "norm_ffn": 1.0 + nrm(ks[8], (DEPTH, D_MODEL), 0.02),
        "gdn_w_in": nrm(ks[9], (N_A_LAYERS, D_MODEL, IN_A_DIM), D_MODEL ** -0.5),
        "gdn_conv_w": nrm(ks[10], (N_A_LAYERS, CONV_W, QKV_DIM), CONV_W ** -0.5),
        "gdn_a_log": jnp.log(jax.random.uniform(ks[11], (N_A_LAYERS, GDN_HEADS), f32, 1.0, 16.0)),
        "gdn_dt_bias": dt + jnp.log(-jnp.expm1(-dt)),
        "gdn_norm": 1.0 + nrm(ks[13], (N_A_LAYERS, GDN_DV), 0.02),
        "gdn_w_out": nrm(ks[14], (N_A_LAYERS, Z_DIM, D_MODEL), Z_DIM ** -0.5),
        "kv_norm": 1.0 + nrm(ks[15], (D_MODEL,), 0.02),
        "w_kv": nrm(ks[16], (D_MODEL, 2 * ATTN_DIM), D_MODEL ** -0.5),
        "moba_w_q": nrm(ks[17], (N_B_LAYERS, D_MODEL, ATTN_DIM), D_MODEL ** -0.5),
        "moba_w_o": nrm(ks[18], (N_B_LAYERS, ATTN_DIM, D_MODEL), ATTN_DIM ** -0.5),
        "moe_w_group": nrm(ks[19], (DEPTH, D_MODEL, N_GROUPS), D_MODEL ** -0.5),
        "moe_b_group": nrm(ks[20], (DEPTH, N_GROUPS), 0.01),
        "moe_w_router": nrm(ks[21], (DEPTH, D_MODEL, N_EXPERTS), D_MODEL ** -0.5),
        "moe_b_router": nrm(ks[22], (DEPTH, N_EXPERTS), 0.01),
        "moe_w_gate": nrm(ks[23], (DEPTH, N_EXPERTS, D_MODEL, EXPERT_FF), D_MODEL ** -0.5),
        "moe_w_up": nrm(ks[24], (DEPTH, N_EXPERTS, D_MODEL, EXPERT_FF), D_MODEL ** -0.5),
        "moe_w_down": nrm(ks[25], (DEPTH, N_EXPERTS, EXPERT_FF, D_MODEL), EXPERT_FF ** -0.5),
        "norm_final": 1.0 + nrm(ks[26], (D_MODEL,), 0.02),
    }


def reference(x_prompt, x_sample, state_gdn, state_conv, cache_k, cache_v, page_table,
              norm_mix, norm_ffn, gdn_w_in, gdn_conv_w, gdn_a_log, gdn_dt_bias, gdn_norm, gdn_w_out,
              kv_norm, w_kv, moba_w_q, moba_w_o,
              moe_w_group, moe_b_group, moe_w_router, moe_b_router, moe_w_gate, moe_w_up, moe_w_down,
              norm_final):
    xp, xs = x_prompt, x_sample
    bsz, s = xp.shape[:2]
    dbsz, t = xs.shape[:2]
    gdn_p, conv_p, gdn_s, conv_s = [], [], [], []
    for l in range(DEPTH):
        hp = rmsnorm(xp, norm_mix[l])
        hs = rmsnorm(xs, norm_mix[l])
        if l < N_A_LAYERS:
            a = l
            conv0 = jnp.zeros((bsz, CONV_W - 1, QKV_DIM), xp.dtype)
            s0 = jnp.zeros((bsz, GDN_HEADS, GDN_DK, GDN_DV), jnp.float32)
            mp, cp, sp = gdn_mixer(hp, conv0, s0, gdn_w_in[a], gdn_conv_w[a], gdn_a_log[a], gdn_dt_bias[a], gdn_norm[a], gdn_w_out[a])
            ms, cs, ss = gdn_mixer(hs, state_conv[a], state_gdn[a], gdn_w_in[a], gdn_conv_w[a], gdn_a_log[a], gdn_dt_bias[a], gdn_norm[a], gdn_w_out[a])
            gdn_p.append(sp.astype(state_gdn.dtype))
            conv_p.append(cp.astype(state_conv.dtype))
            gdn_s.append(ss.astype(state_gdn.dtype))
            conv_s.append(cs.astype(state_conv.dtype))
        else:
            j = l - N_A_LAYERS
            qp = (hp @ moba_w_q[j]).reshape(bsz, s, MOBA_HEADS, MOBA_HD)
            qs = (hs @ moba_w_q[j]).reshape(dbsz, t, MOBA_HEADS, MOBA_HD)
            mp = moba_attend_prompt(qp, kmean_p, kbh_p, vbh_p) @ moba_w_o[j]
            ms = moba_attend_sample(qs, bmean_s, k_own_s, v_own_s, cache_k, cache_v, page_table) @ moba_w_o[j]
        xp = xp + mp
        xs = xs + ms
        xp = xp + hier_moe(rmsnorm(xp, norm_ffn[l]), moe_w_group[l], moe_b_group[l], moe_w_router[l], moe_b_router[l], moe_w_gate[l], moe_w_up[l], moe_w_down[l])
        xs = xs + hier_moe(rmsnorm(xs, norm_ffn[l]), moe_w_group[l], moe_b_group[l], moe_w_router[l], moe_b_router[l], moe_w_gate[l], moe_w_up[l], moe_w_down[l])
        if l == N_A_LAYERS - 1:
            kvp = rmsnorm(xp, kv_norm) @ w_kv
            kvs = rmsnorm(xs, kv_norm) @ w_kv
            k_p = kvp[..., :ATTN_DIM].reshape(bsz, s, MOBA_HEADS, MOBA_HD)
            v_p = kvp[..., ATTN_DIM:].reshape(bsz, s, MOBA_HEADS, MOBA_HD)
            k_s = kvs[..., :ATTN_DIM].reshape(dbsz, t, MOBA_HEADS, MOBA_HD)
            v_s = kvs[..., ATTN_DIM:].reshape(dbsz, t, MOBA_HEADS, MOBA_HD)
            kmean_p, kbh_p, vbh_p = moba_context_prompt(k_p, v_p)
            bmean_s, k_own_s, v_own_s = moba_context_sample(k_s, v_s, cache_k, cache_v, page_table)
    y_prompt = rmsnorm(xp, norm_final)
    y_sample = rmsnorm(xs, norm_final)
    return (y_prompt, y_sample, jnp.stack(gdn_p), jnp.stack(conv_p), k_p, v_p,
            jnp.stack(gdn_s), jnp.stack(conv_s), k_s, v_s)
```

```python
import functools

import jax
import jax.numpy as jnp
from jax import lax
from jax.experimental import pallas as pl
from jax.experimental.pallas import tpu as pltpu

RMS_EPS = 1e-6
L2_EPS = 1e-6
MOBA_BLOCK = 256
MOBA_TOPK = 3
TOPK_IN_GROUP = 2
GDN_CHUNK = 64
CONV_PAD_ROWS = 8
LANES = 128
VMEM_LIMIT = 56 * 1024 * 1024

F32 = jnp.float32
BF16 = jnp.bfloat16
HIGHEST = lax.Precision.HIGHEST
NEG_INF = float("-inf")


def _params(*sem):
    return pltpu.CompilerParams(dimension_semantics=sem, vmem_limit_bytes=VMEM_LIMIT)


def _rms(x, g):
    return x * lax.rsqrt(jnp.mean(x * x, axis=-1, keepdims=True) + RMS_EPS) * g


def _silu(x):
    return x * jax.nn.sigmoid(x)


def _softplus(x):
    return jnp.maximum(x, 0.0) + jnp.log1p(jnp.exp(-jnp.abs(x)))


def _dot(a, b):
    return jnp.dot(a, b, preferred_element_type=F32)


def _dot_nt(a, b, precision=None):
    return lax.dot_general(a, b, (((1,), (1,)), ((), ())), precision=precision, preferred_element_type=F32)


def _dot_tn(a, b):
    return lax.dot_general(a, b, (((0,), (0,)), ((), ())), preferred_element_type=F32)


def _norm_matmul_kernel(x_ref, g_ref, w_ref, o_ref, xn_ref):
    @pl.when(pl.program_id(1) == 0)
    def _():
        xn_ref[...] = _rms(x_ref[...], g_ref[...]).astype(BF16)

    o_ref[...] = _dot(xn_ref[...], w_ref[...])


def norm_matmul(x, g, w, *, tm, tn):
    n, d = x.shape
    m = w.shape[1]
    assert n % tm == 0 and m % tn == 0
    return pl.pallas_call(
        _norm_matmul_kernel,
        out_shape=jax.ShapeDtypeStruct((n, m), F32),
        grid=(n // tm, m // tn),
        in_specs=[
            pl.BlockSpec((tm, d), lambda i, j: (i, 0)),
            pl.BlockSpec((1, d), lambda i, j: (0, 0)),
            pl.BlockSpec((d, tn), lambda i, j: (0, j)),
        ],
        out_specs=pl.BlockSpec((tm, tn), lambda i, j: (i, j)),
        scratch_shapes=[pltpu.VMEM((tm, d), BF16)],
        compiler_params=_params("parallel", "arbitrary"),
        name="norm_matmul",
    )(x, g.reshape(1, d), w)


def _matmul_residual_kernel(a_ref, w_ref, r_ref, o_ref):
    o_ref[...] = r_ref[...] + _dot(a_ref[...].astype(BF16), w_ref[...])


def matmul_residual(a, w, resid, *, tm, tn):
    n, k = a.shape
    m = w.shape[1]
    assert n % tm == 0 and m % tn == 0
    return pl.pallas_call(
        _matmul_residual_kernel,
        out_shape=jax.ShapeDtypeStruct((n, m), F32),
        grid=(n // tm, m // tn),
        in_specs=[
            pl.BlockSpec((tm, k), lambda i, j: (i, 0)),
            pl.BlockSpec((k, tn), lambda i, j: (0, j)),
            pl.BlockSpec((tm, tn), lambda i, j: (i, j)),
        ],
        out_specs=pl.BlockSpec((tm, tn), lambda i, j: (i, j)),
        compiler_params=_params("parallel", "parallel"),
        name="matmul_residual",
    )(a, w, resid)


def _rmsnorm_kernel(x_ref, g_ref, o_ref):
    o_ref[...] = _rms(x_ref[...], g_ref[...])


def rmsnorm(x, g, *, tm):
    n, d = x.shape
    assert n % tm == 0
    return pl.pallas_call(
        _rmsnorm_kernel,
        out_shape=jax.ShapeDtypeStruct((n, d), F32),
        grid=(n // tm,),
        in_specs=[pl.BlockSpec((tm, d), lambda i: (i, 0)), pl.BlockSpec((1, d), lambda i: (0, 0))],
        out_specs=pl.BlockSpec((tm, d), lambda i: (i, 0)),
        compiler_params=_params("parallel"),
        name="rmsnorm",
    )(x, g.reshape(1, d))


def _gdn_kernel(qkv_ref, z_ref, ba_ref, cprev_ref, s0_ref, cw_ref, par_ref, ng_ref,
                og_ref, s_ref, xbuf_ref, *, chunk, heads, dk, dv, t_valid):
    c = pl.program_id(1)
    hk = heads * dk
    hist = CONV_PAD_ROWS
    conv_w = cw_ref.shape[0]

    @pl.when(c == 0)
    def _():
        xbuf_ref[0:hist, :] = cprev_ref[0]
        s_ref[...] = s0_ref[...]

    xbuf_ref[hist:hist + chunk, :] = qkv_ref[0]

    row = lax.broadcasted_iota(jnp.int32, (chunk, LANES), 0)
    lane = lax.broadcasted_iota(jnp.int32, (chunk, LANES), 1)
    valid = (row + c * chunk) < t_valid
    ba = ba_ref[0]
    neg_a = -jnp.exp(par_ref[0:1, :])
    dt_bias = par_ref[1:2, :]
    beta_all = jnp.where(valid, jax.nn.sigmoid(ba), 0.0)
    g_all = jnp.where(valid, neg_a * _softplus(ba + dt_bias), 0.0)

    r_i = lax.broadcasted_iota(jnp.int32, (chunk, chunk), 0)
    c_i = lax.broadcasted_iota(jnp.int32, (chunk, chunk), 1)
    incl = c_i <= r_i
    strict = c_i < r_i
    gcum_all = jnp.dot(incl.astype(F32), g_all, precision=HIGHEST, preferred_element_type=F32)

    def conv_silu(col, width):
        acc = None
        for i in range(conv_w):
            lo = hist - (conv_w - 1) + i
            term = xbuf_ref[lo:lo + chunk, col:col + width] * cw_ref[i:i + 1, col:col + width]
            acc = term if acc is None else acc + term
        return _silu(acc)

    def l2n(x):
        return x * lax.rsqrt(jnp.sum(x * x, axis=-1, keepdims=True) + L2_EPS)

    for h in range(heads):
        q = l2n(conv_silu(h * dk, dk)) * (dk ** -0.5)
        k = l2n(conv_silu(hk + h * dk, dk))
        v = conv_silu(2 * hk + h * dv, dv)
        beta = beta_all[:, h:h + 1]
        gc = gcum_all[:, heads + h:heads + h + 1]
        pick = (lane == heads + h).astype(F32)
        gr = _dot_nt(pick, gcum_all, precision=HIGHEST)[:, :chunk]
        decay = jnp.exp(jnp.where(incl, gc - gr, NEG_INF))
        egc = jnp.exp(gc)
        s = s_ref[0, h]
        s_bf = s.astype(BF16)
        q_bf = q.astype(BF16)
        k_bf = k.astype(BF16)
        kk = _dot_nt(k_bf, k_bf)
        m = jnp.where(strict, beta * decay * kk, 0.0)
        u = beta * (v - egc * _dot(k_bf, s_bf))
        powers = [m.astype(BF16)]
        span = 2
        while span < chunk:
            powers.append(_dot(powers[-1], powers[-1]).astype(BF16))
            span *= 2
        for p in reversed(powers[1:]):
            u = u + _dot(p, u.astype(BF16))
        u = u - _dot(powers[0], u.astype(BF16))
        u_bf = u.astype(BF16)
        qk = _dot_nt(q_bf, k_bf) * decay
        o = egc * _dot(q_bf, s_bf) + _dot(qk.astype(BF16), u_bf)
        glast = gc[chunk - 1:chunk, :]
        kd = (k * jnp.exp(glast - gc)).astype(BF16)
        s_ref[0, h] = jnp.exp(glast) * s + _dot_tn(kd, u_bf)
        zg = _silu(z_ref[0, :, h * dv:(h + 1) * dv])
        on = o * lax.rsqrt(jnp.mean(o * o, axis=-1, keepdims=True) + RMS_EPS)
        og_ref[0, :, h * dv:(h + 1) * dv] = on * ng_ref[...] * zg

    xbuf_ref[0:hist, :] = xbuf_ref[chunk:chunk + hist, :]


def gdn_core(proj, conv_prev, s0, conv_w, a_log, dt_bias, norm_g, *, t_valid):
    bsz, tp, _ = proj.shape
    heads = a_log.shape[0]
    dv = norm_g.shape[0]
    qkv_dim = conv_w.shape[1]
    dk = (qkv_dim // heads - dv) // 2
    z_dim = heads * dv
    chunk = GDN_CHUNK
    assert tp % chunk == 0 and dk == LANES and dv == LANES and 2 * heads <= LANES
    assert conv_w.shape[0] - 1 <= CONV_PAD_ROWS <= chunk
    par = jnp.zeros((8, LANES), F32)
    par = par.at[0, heads:2 * heads].set(a_log).at[1, heads:2 * heads].set(dt_bias)
    cprev = jnp.pad(conv_prev, ((0, 0), (CONV_PAD_ROWS - conv_prev.shape[1], 0), (0, 0)))
    nq = qkv_dim // LANES
    kern = functools.partial(_gdn_kernel, chunk=chunk, heads=heads, dk=dk, dv=dv, t_valid=t_valid)
    return pl.pallas_call(
        kern,
        out_shape=(jax.ShapeDtypeStruct((bsz, tp, z_dim), F32),
                   jax.ShapeDtypeStruct((bsz, heads, dk, dv), F32)),
        grid=(bsz, tp // chunk),
        in_specs=[
            pl.BlockSpec((1, chunk, qkv_dim), lambda b, c: (b, c, 0)),
            pl.BlockSpec((1, chunk, z_dim), lambda b, c: (b, c, qkv_dim // z_dim)),
            pl.BlockSpec((1, chunk, LANES), lambda b, c: (b, c, nq + z_dim // LANES)),
            pl.BlockSpec((1, CONV_PAD_ROWS, qkv_dim), lambda b, c: (b, 0, 0)),
            pl.BlockSpec((1, heads, dk, dv), lambda b, c: (b, 0, 0, 0)),
            pl.BlockSpec(conv_w.shape, lambda b, c: (0, 0)),
            pl.BlockSpec((8, LANES), lambda b, c: (0, 0)),
            pl.BlockSpec((1, dv), lambda b, c: (0, 0)),
        ],
        out_specs=(pl.BlockSpec((1, chunk, z_dim), lambda b, c: (b, c, 0)),
                   pl.BlockSpec((1, heads, dk, dv), lambda b, c: (b, 0, 0, 0))),
        scratch_shapes=[pltpu.VMEM((CONV_PAD_ROWS + chunk, qkv_dim), F32)],
        compiler_params=_params("parallel", "arbitrary"),
        name="gdn_core",
    )(proj, proj, proj, cprev, s0, conv_w, par, norm_g.reshape(1, dv))


def _moe_kernel(x_ref, g_ref, wr_ref, br_ref, wg_ref, wu_ref, wd_ref, o_ref, xn_ref, comb_ref,
                *, n_groups, n_experts):
    e = pl.program_id(1)
    tm = x_ref.shape[0]
    epg = n_experts // n_groups
    lane = lax.broadcasted_iota(jnp.int32, (tm, LANES), 1)

    @pl.when(e == 0)
    def _():
        x = x_ref[...]
        xn = _rms(x, g_ref[...])
        xn_ref[...] = xn.astype(BF16)
        logits = jnp.dot(xn, wr_ref[...], precision=HIGHEST, preferred_element_type=F32) + br_ref[...]
        is_g = lane < n_groups
        gl = jnp.where(is_g, logits, NEG_INF)
        ge = jnp.exp(gl - jnp.max(gl, axis=-1, keepdims=True))
        pg = ge / jnp.sum(ge, axis=-1, keepdims=True)
        pg_sel = jnp.max(pg, axis=-1, keepdims=True)
        g_idx = jnp.min(jnp.where(is_g & (pg == pg_sel), lane, LANES), axis=-1, keepdims=True)
        lo = n_groups + g_idx * epg
        in_g = (lane >= lo) & (lane < lo + epg)
        el = jnp.where(in_g, logits, NEG_INF)
        ee = jnp.exp(el - jnp.max(el, axis=-1, keepdims=True))
        pe = ee / jnp.sum(ee, axis=-1, keepdims=True)
        comb = jnp.zeros((tm, LANES), F32)
        tops = []
        rest = jnp.where(in_g, pe, NEG_INF)
        for _ in range(TOPK_IN_GROUP):
            top_v = jnp.max(rest, axis=-1, keepdims=True)
            top_i = jnp.min(jnp.where(rest == top_v, lane, LANES), axis=-1, keepdims=True)
            tops.append((top_v, top_i))
            rest = jnp.where(lane == top_i, NEG_INF, rest)
        denom = tops[0][0]
        for top_v, _ in tops[1:]:
            denom = denom + top_v
        for top_v, top_i in tops:
            comb = comb + jnp.where(lane == top_i, pg_sel * top_v / denom, 0.0)
        comb_ref[...] = comb
        o_ref[...] = x

    xn = xn_ref[...]
    he = _silu(_dot(xn, wg_ref[0])) * _dot(xn, wu_ref[0])
    y = _dot(he.astype(BF16), wd_ref[0])
    ce = jnp.sum(jnp.where(lane == n_groups + e, comb_ref[...], 0.0), axis=-1, keepdims=True)
    o_ref[...] += ce * y


def hier_moe_residual(x, g, w_route, b_route, w_gate, w_up, w_down, *, n_groups, tm):
    n, d = x.shape
    n_experts, _, ff = w_gate.shape
    assert n % tm == 0 and n_groups + n_experts <= LANES
    kern = functools.partial(_moe_kernel, n_groups=n_groups, n_experts=n_experts)
    return pl.pallas_call(
        kern,
        out_shape=jax.ShapeDtypeStruct((n, d), F32),
        grid=(n // tm, n_experts),
        in_specs=[
            pl.BlockSpec((tm, d), lambda i, e: (i, 0)),
            pl.BlockSpec((1, d), lambda i, e: (0, 0)),
            pl.BlockSpec((d, LANES), lambda i, e: (0, 0)),
            pl.BlockSpec((1, LANES), lambda i, e: (0, 0)),
            pl.BlockSpec((1, d, ff), lambda i, e: (e, 0, 0)),
            pl.BlockSpec((1, d, ff), lambda i, e: (e, 0, 0)),
            pl.BlockSpec((1, ff, d), lambda i, e: (e, 0, 0)),
        ],
        out_specs=pl.BlockSpec((tm, d), lambda i, e: (i, 0)),
        scratch_shapes=[pltpu.VMEM((tm, d), BF16), pltpu.VMEM((tm, LANES), F32)],
        compiler_params=_params("parallel", "arbitrary"),
        name="hier_moe",
    )(x, g.reshape(1, d), w_route, b_route, w_gate, w_up, w_down)


def _moba_prompt_kernel(q_ref, k_ref, v_ref, o_ref, kmean_ref, lg_ref, *, n_blocks, n_sel, scale):
    qi = pl.program_id(2)
    blk = MOBA_BLOCK

    @pl.when(qi == 0)
    def _():
        kmean_ref[...] = jnp.zeros_like(kmean_ref)
        for j in range(n_blocks):
            kmean_ref[j:j + 1, :] = jnp.mean(k_ref[0, j * blk:(j + 1) * blk, :], axis=0, keepdims=True)

    q = q_ref[0]
    q_bf = q.astype(BF16)
    gate = _dot_nt(q, kmean_ref[...], precision=HIGHEST)
    cols = [gate[:, j:j + 1] for j in range(n_blocks)]
    sel = []
    for j in range(n_blocks):
        rank = jnp.zeros((blk, 1), jnp.int32)
        for i in range(n_blocks):
            if i == j:
                continue
            ahead = (cols[i] >= cols[j]) if i < j else (cols[i] > cols[j])
            rank = rank + jnp.where(ahead, 1, 0) * (i < qi).astype(jnp.int32)
        sel.append(rank < n_sel)

    r_i = lax.broadcasted_iota(jnp.int32, (blk, blk), 0)
    c_i = lax.broadcasted_iota(jnp.int32, (blk, blk), 1)
    causal = c_i <= r_i

    def logits(j):
        return _dot_nt(q_bf, k_ref[0, j * blk:(j + 1) * blk, :].astype(BF16)) * scale

    for j in range(n_blocks):
        @pl.when(j < qi)
        def _(j=j):
            lg_ref[:, j * blk:(j + 1) * blk] = jnp.where(sel[j], logits(j), NEG_INF)

        @pl.when(j == qi)
        def _(j=j):
            lg_ref[:, j * blk:(j + 1) * blk] = jnp.where(causal, logits(j), NEG_INF)

        @pl.when(j > qi)
        def _(j=j):
            lg_ref[:, j * blk:(j + 1) * blk] = jnp.full((blk, blk), NEG_INF, F32)

    row_max = jnp.max(lg_ref[...], axis=-1, keepdims=True)

    def body(j, carry):
        den, out = carry
        start = pl.multiple_of(j * blk, blk)
        p = jnp.exp(lg_ref[:, pl.ds(start, blk)] - row_max)
        den = den + jnp.sum(p, axis=-1, keepdims=True)
        out = out + _dot(p.astype(BF16), v_ref[0, pl.ds(start, blk), :].astype(BF16))
        return den, out

    den, out = lax.fori_loop(0, qi + 1, body,
                             (jnp.zeros((blk, 1), F32), jnp.zeros((blk, o_ref.shape[2]), F32)))
    o_ref[0] = out / den


def moba_prompt(q, k, v, *, heads):
    bsz, s, hd_all = q.shape
    hd = hd_all // heads
    assert s % MOBA_BLOCK == 0 and hd == LANES
    n_blocks = s // MOBA_BLOCK
    assert n_blocks <= LANES - 8
    n_sel = min(MOBA_TOPK, (s - 1) // MOBA_BLOCK)
    kern = functools.partial(_moba_prompt_kernel, n_blocks=n_blocks, n_sel=n_sel, scale=hd ** -0.5)
    return pl.pallas_call(
        kern,
        out_shape=jax.ShapeDtypeStruct((bsz, s, hd_all), F32),
        grid=(bsz, heads, n_blocks),
        in_specs=[
            pl.BlockSpec((1, MOBA_BLOCK, hd), lambda b, h, i: (b, i, h)),
            pl.BlockSpec((1, s, hd), lambda b, h, i: (b, 0, h)),
            pl.BlockSpec((1, s, hd), lambda b, h, i: (b, 0, h)),
        ],
        out_specs=pl.BlockSpec((1, MOBA_BLOCK, hd), lambda b, h, i: (b, i, h)),
        scratch_shapes=[pltpu.VMEM((LANES, hd), F32), pltpu.VMEM((MOBA_BLOCK, s), F32)],
        compiler_params=_params("parallel", "parallel", "arbitrary"),
        name="moba_prompt",
    )(q, k, v)


def _block_means_kernel(pt_ref, *refs, n_pages_step, ppb):
    del pt_ref
    o_ref = refs[n_pages_step]
    for n in range(n_pages_step // ppb):
        acc = None
        for p in range(ppb):
            part = jnp.sum(refs[n * ppb + p][0], axis=0, keepdims=True)
            acc = part if acc is None else acc + part
        o_ref[0, n:n + 1, :] = acc / MOBA_BLOCK


def paged_block_means(cache_k, page_table, *, n_full, ppb):
    n_pool, page, width = cache_k.shape
    dbsz = page_table.shape[0]
    blocks_step = 8
    assert n_full % blocks_step == 0
    n_pages_step = blocks_step * ppb
    kern = functools.partial(_block_means_kernel, n_pages_step=n_pages_step, ppb=ppb)

    def page_spec(i):
        return pl.BlockSpec((1, page, width), lambda b, s, pt: (pt[b, s * n_pages_step + i], 0, 0))

    return pl.pallas_call(
        kern,
        out_shape=jax.ShapeDtypeStruct((dbsz, n_full, width), F32),
        grid_spec=pltpu.PrefetchScalarGridSpec(
            num_scalar_prefetch=1,
            grid=(dbsz, n_full // blocks_step),
            in_specs=[page_spec(i) for i in range(n_pages_step)],
            out_specs=pl.BlockSpec((1, blocks_step, width), lambda b, s, pt: (b, s, 0)),
        ),
        compiler_params=_params("parallel", "parallel"),
        name="paged_block_means",
    )(page_table, *([cache_k] * n_pages_step))


def _sample_select_kernel(q_ref, bm_ref, pt_ref, o_ref, *, heads, n_sel, ppb):
    rows, hd = q_ref.shape[1], q_ref.shape[2]
    nb = bm_ref.shape[1]
    n_pages = pt_ref.shape[2]
    q = q_ref[0]
    row_head = lax.broadcasted_iota(jnp.int32, (rows, nb), 0) % heads
    lane = lax.broadcasted_iota(jnp.int32, (rows, nb), 1)
    lane_pt = lax.broadcasted_iota(jnp.int32, (rows, n_pages), 1)
    lane_out = lax.broadcasted_iota(jnp.int32, (rows, LANES), 1)
    gate = jnp.zeros((rows, nb), F32)
    for h in range(heads):
        gate_h = _dot_nt(q, bm_ref[0, :, h * hd:(h + 1) * hd], precision=HIGHEST)
        gate = jnp.where(row_head == h, gate_h, gate)
    pt = pt_ref[0].astype(F32)
    out = jnp.zeros((rows, LANES), F32)
    for s in range(n_sel):
        top = jnp.max(gate, axis=-1, keepdims=True)
        idx = jnp.min(jnp.where(gate == top, lane, nb), axis=-1, keepdims=True)
        gate = jnp.where(lane == idx, NEG_INF, gate)
        for p in range(ppb):
            phys = jnp.sum(jnp.where(lane_pt == idx * ppb + p, pt, 0.0), axis=-1, keepdims=True)
            out = out + jnp.where(lane_out == s * ppb + p, phys, 0.0)
    o_ref[0] = out.astype(jnp.int32)


def sample_select(q, bmean, page_table, *, heads, n_sel, ppb):
    dbsz, rows, hd = q.shape
    nb = bmean.shape[1]
    n_pages = page_table.shape[1]
    kern = functools.partial(_sample_select_kernel, heads=heads, n_sel=n_sel, ppb=ppb)
    return pl.pallas_call(
        kern,
        out_shape=jax.ShapeDtypeStruct((dbsz, rows, LANES), jnp.int32),
        grid=(dbsz,),
        in_specs=[
            pl.BlockSpec((1, rows, hd), lambda b: (b, 0, 0)),
            pl.BlockSpec((1, nb, heads * hd), lambda b: (b, 0, 0)),
            pl.BlockSpec((1, 1, n_pages), lambda b: (b, 0, 0)),
        ],
        out_specs=pl.BlockSpec((1, rows, LANES), lambda b: (b, 0, 0)),
        compiler_params=_params("parallel"),
        name="sample_select",
    )(q, bmean, page_table.reshape(dbsz, 1, n_pages))


def _sample_attend_kernel(phys_ref, q_ref, kn_ref, vn_ref, *refs, n_slabs, heads, t_len, scale):
    del phys_ref
    k_refs, v_refs, o_ref = refs[:n_slabs], refs[n_slabs:2 * n_slabs], refs[2 * n_slabs]
    t = (pl.program_id(0) // heads) % t_len
    hd = q_ref.shape[2]
    q8 = jnp.broadcast_to(q_ref[0], (8, hd)).astype(BF16)
    own = _dot_nt(q8, kn_ref[0].astype(BF16)) * scale
    lane = lax.broadcasted_iota(jnp.int32, own.shape, 1)
    own = jnp.where(lane <= t, own, NEG_INF)
    past = [_dot_nt(q8, k_refs[s][0].astype(BF16)) * scale for s in range(n_slabs)]
    top = jnp.max(own, axis=-1, keepdims=True)
    for lg in past:
        top = jnp.maximum(top, jnp.max(lg, axis=-1, keepdims=True))
    p_own = jnp.exp(own - top)
    den = jnp.sum(p_own, axis=-1, keepdims=True)
    out = _dot(p_own.astype(BF16), vn_ref[0].astype(BF16))
    for s in range(n_slabs):
        p = jnp.exp(past[s] - top)
        den = den + jnp.sum(p, axis=-1, keepdims=True)
        out = out + _dot(p.astype(BF16), v_refs[s][0].astype(BF16))
    o_ref[0] = (out / den)[0:1]


def sample_attend(q, k_own, v_own, cache_k, cache_v, phys, *, heads, t_len, n_slabs):
    n, _, hd = q.shape
    page = cache_k.shape[1]
    kern = functools.partial(_sample_attend_kernel, n_slabs=n_slabs, heads=heads, t_len=t_len, scale=hd ** -0.5)

    def slab_spec(s):
        return pl.BlockSpec((1, page, hd), lambda i, ph: (ph[i * n_slabs + s], 0, i % heads))

    own_spec = pl.BlockSpec((1, page, hd), lambda i, ph: (i // (t_len * heads), 0, i % heads))
    return pl.pallas_call(
        kern,
        out_shape=jax.ShapeDtypeStruct((n, 1, hd), F32),
        grid_spec=pltpu.PrefetchScalarGridSpec(
            num_scalar_prefetch=1,
            grid=(n,),
            in_specs=[pl.BlockSpec((1, 1, hd), lambda i, ph: (i, 0, 0)), own_spec, own_spec]
            + [slab_spec(s) for s in range(n_slabs)] * 2,
            out_specs=pl.BlockSpec((1, 1, hd), lambda i, ph: (i, 0, 0)),
        ),
        compiler_params=_params("arbitrary"),
        name="sample_attend",
    )(phys, q, k_own, v_own, *([cache_k] * n_slabs), *([cache_v] * n_slabs))


def _row_tile(n, target):
    tm = min(n, target)
    assert n % tm == 0
    return tm


def kernel(x_prompt, x_sample, state_gdn, state_conv, cache_k, cache_v, page_table, norm_mix, norm_ffn,
           gdn_w_in, gdn_conv_w, gdn_a_log, gdn_dt_bias, gdn_norm, gdn_w_out, kv_norm, w_kv, moba_w_q,
           moba_w_o, moe_w_group, moe_b_group, moe_w_router, moe_b_router, moe_w_gate, moe_w_up,
           moe_w_down, norm_final):
    bsz, s, d = x_prompt.shape
    dbsz, t, _ = x_sample.shape
    depth = norm_mix.shape[0]
    n_a = gdn_w_in.shape[0]
    conv_w, qkv_dim = gdn_conv_w.shape[1:]
    gdn_heads, dv = gdn_a_log.shape[1], gdn_norm.shape[1]
    z_dim = gdn_heads * dv
    n_pool, page, moba_heads, hd = cache_k.shape
    attn_dim = moba_heads * hd
    n_pages = page_table.shape[1]
    ppb = MOBA_BLOCK // page
    n_full = n_pages // ppb
    n_groups = moe_w_group.shape[-1]
    n_experts = moe_w_router.shape[-1]
    assert n_pages == n_full * ppb, "a partial past block (tail pages) is not supported"
    assert n_full >= MOBA_TOPK and t >= conv_w - 1 and s >= conv_w - 1 and t <= page

    groups = [(x_prompt.reshape(bsz * s, d), bsz, s), (x_sample.reshape(dbsz * t, d), dbsz, t)]
    tiles = [_row_tile(bsz * s, 512), _row_tile(dbsz * t, 512)]
    cache_k3 = cache_k.reshape(n_pool, page, attn_dim)
    cache_v3 = cache_v.reshape(n_pool, page, attn_dim)

    def moe(l, x, tm):
        w_route = jnp.zeros((d, LANES), F32).at[:, :n_groups].set(moe_w_group[l])
        w_route = w_route.at[:, n_groups:n_groups + n_experts].set(moe_w_router[l])
        b_route = jnp.zeros((1, LANES), F32).at[0, :n_groups].set(moe_b_group[l])
        b_route = b_route.at[0, n_groups:n_groups + n_experts].set(moe_b_router[l])
        return hier_moe_residual(x, norm_ffn[l], w_route, b_route, moe_w_gate[l].astype(BF16),
                                 moe_w_up[l].astype(BF16), moe_w_down[l].astype(BF16),
                                 n_groups=n_groups, tm=tm)

    xs = [g[0] for g in groups]
    gdn_states, conv_states, kvs = [[], []], [[], []], [None, None]
    for l in range(depth):
        if l < n_a:
            in_dim = gdn_w_in.shape[2]
            in_pad = -(-in_dim // (3 * LANES)) * (3 * LANES)
            assert qkv_dim + z_dim + LANES <= in_pad and in_dim - qkv_dim - z_dim == 2 * gdn_heads
            w_in = jnp.pad(gdn_w_in[l], ((0, 0), (0, in_pad - in_dim))).astype(BF16)
            w_out = gdn_w_out[l].astype(BF16)
            for gi, (_, nb, tl) in enumerate(groups):
                tm = tiles[gi]
                proj = norm_matmul(xs[gi], norm_mix[l], w_in, tm=tm, tn=3 * LANES).reshape(nb, tl, in_pad)
                tp = -(-tl // GDN_CHUNK) * GDN_CHUNK
                proj_p = jnp.pad(proj, ((0, 0), (0, tp - tl), (0, 0))) if tp != tl else proj
                if gi == 0:
                    conv0 = jnp.zeros((nb, conv_w - 1, qkv_dim), F32)
                    s0 = jnp.zeros((nb, gdn_heads, (qkv_dim // gdn_heads - dv) // 2, dv), F32)
                else:
                    conv0, s0 = state_conv[l], state_gdn[l]
                og, s_fin = gdn_core(proj_p, conv0, s0, gdn_conv_w[l], gdn_a_log[l], gdn_dt_bias[l],
                                     gdn_norm[l], t_valid=tl)
                og = og[:, :tl].reshape(nb * tl, z_dim)
                gdn_states[gi].append(s_fin)
                conv_states[gi].append(proj[:, tl - (conv_w - 1):, :qkv_dim])
                xs[gi] = matmul_residual(og, w_out, xs[gi], tm=tm, tn=512)
        else:
            j = l - n_a
            w_q = moba_w_q[j].astype(BF16)
            w_o = moba_w_o[j].astype(BF16)
            q = norm_matmul(xs[0], norm_mix[l], w_q, tm=tiles[0], tn=512).reshape(bsz, s, attn_dim)
            k_p, v_p = kvs[0]
            o = moba_prompt(q, k_p.reshape(bsz, s, attn_dim), v_p.reshape(bsz, s, attn_dim), heads=moba_heads)
            xs[0] = matmul_residual(o.reshape(bsz * s, attn_dim), w_o, xs[0], tm=tiles[0], tn=512)
            q = norm_matmul(xs[1], norm_mix[l], w_q, tm=tiles[1], tn=512)
            k_s, v_s = kvs[1]
            n_sel = min(MOBA_TOPK, n_full)
            phys = sample_select(q.reshape(dbsz, t * moba_heads, hd), bmean, page_table,
                                 heads=moba_heads, n_sel=n_sel, ppb=ppb)
            phys = phys[:, :, :n_sel * ppb].reshape(-1)
            pad_rows = ((0, 0), (0, page - t), (0, 0))
            o = sample_attend(q.reshape(dbsz * t * moba_heads, 1, hd),
                              jnp.pad(k_s.reshape(dbsz, t, attn_dim), pad_rows),
                              jnp.pad(v_s.reshape(dbsz, t, attn_dim), pad_rows),
                              cache_k3, cache_v3, phys, heads=moba_heads, t_len=t, n_slabs=n_sel * ppb)
            xs[1] = matmul_residual(o.reshape(dbsz * t, attn_dim), w_o, xs[1], tm=tiles[1], tn=512)
        for gi in range(2):
            xs[gi] = moe(l, xs[gi], tiles[gi])
        if l == n_a - 1:
            w_kv_bf = w_kv.astype(BF16)
            for gi, (_, nb, tl) in enumerate(groups):
                kv = norm_matmul(xs[gi], kv_norm, w_kv_bf, tm=tiles[gi], tn=512)
                kvs[gi] = (kv[:, :attn_dim].reshape(nb, tl, moba_heads, hd),
                           kv[:, attn_dim:].reshape(nb, tl, moba_heads, hd))
            bmean = paged_block_means(cache_k3, page_table, n_full=n_full, ppb=ppb)
    y_p = rmsnorm(xs[0], norm_final, tm=tiles[0]).reshape(bsz, s, d)
    y_s = rmsnorm(xs[1], norm_final, tm=tiles[1]).reshape(dbsz, t, d)
    return (y_p, y_s, jnp.stack(gdn_states[0]), jnp.stack(conv_states[0]), kvs[0][0], kvs[0][1],
            jnp.stack(gdn_states[1]), jnp.stack(conv_states[1]), kvs[1][0], kvs[1][1])
```

```python
import functools

import jax
import jax.numpy as jnp
from jax import lax
from jax.experimental import pallas as pl
from jax.experimental.pallas import tpu as pltpu

RMS_EPS = 1e-6
L2_EPS = 1e-6
MOBA_BLOCK = 256
MOBA_TOPK = 3
TOPK_IN_GROUP = 2
GDN_CHUNK = 64
CONV_PAD_ROWS = 8
LANES = 128
VMEM_LIMIT = 56 * 1024 * 1024

F32 = jnp.float32
BF16 = jnp.bfloat16
HIGHEST = lax.Precision.HIGHEST
NEG_INF = float("-inf")


def _params(*sem):
    return pltpu.CompilerParams(dimension_semantics=sem, vmem_limit_bytes=VMEM_LIMIT)


def _rms(x, g):
    return x * lax.rsqrt(jnp.mean(x * x, axis=-1, keepdims=True) + RMS_EPS) * g


def _silu(x):
    return x * jax.nn.sigmoid(x)


def _softplus(x):
    return jnp.maximum(x, 0.0) + jnp.log1p(jnp.exp(-jnp.abs(x)))


def _dot(a, b):
    return jnp.dot(a, b, preferred_element_type=F32)


def _dot_nt(a, b, precision=None):
    return lax.dot_general(a, b, (((1,), (1,)), ((), ())), precision=precision, preferred_element_type=F32)


def _dot_tn(a, b):
    return lax.dot_general(a, b, (((0,), (0,)), ((), ())), preferred_element_type=F32)


def _norm_matmul_kernel(x_ref, g_ref, w_ref, o_ref, xn_ref):
    @pl.when(pl.program_id(1) == 0)
    def _():
        xn_ref[...] = _rms(x_ref[...], g_ref[...]).astype(BF16)

    o_ref[...] = _dot(xn_ref[...], w_ref[...])


def norm_matmul(x, g, w, *, tm, tn):
    n, d = x.shape
    m = w.shape[1]
    assert n % tm == 0 and m % tn == 0
    return pl.pallas_call(
        _norm_matmul_kernel,
        out_shape=jax.ShapeDtypeStruct((n, m), F32),
        grid=(n // tm, m // tn),
        in_specs=[
            pl.BlockSpec((tm, d), lambda i, j: (i, 0)),
            pl.BlockSpec((1, d), lambda i, j: (0, 0)),
            pl.BlockSpec((d, tn), lambda i, j: (0, j)),
        ],
        out_specs=pl.BlockSpec((tm, tn), lambda i, j: (i, j)),
        scratch_shapes=[pltpu.VMEM((tm, d), BF16)],
        compiler_params=_params("parallel", "arbitrary"),
        name="norm_matmul",
    )(x, g.reshape(1, d), w)


def _matmul_residual_kernel(a_ref, w_ref, r_ref, o_ref):
    o_ref[...] = r_ref[...] + _dot(a_ref[...].astype(BF16), w_ref[...])


def matmul_residual(a, w, resid, *, tm, tn):
    n, k = a.shape
    m = w.shape[1]
    assert n % tm == 0 and m % tn == 0
    return pl.pallas_call(
        _matmul_residual_kernel,
        out_shape=jax.ShapeDtypeStruct((n, m), F32),
        grid=(n // tm, m // tn),
        in_specs=[
            pl.BlockSpec((tm, k), lambda i, j: (i, 0)),
            pl.BlockSpec((k, tn), lambda i, j: (0, j)),
            pl.BlockSpec((tm, tn), lambda i, j: (i, j)),
        ],
        out_specs=pl.BlockSpec((tm, tn), lambda i, j: (i, j)),
        compiler_params=_params("parallel", "parallel"),
        name="matmul_residual",
    )(a, w, resid)


def _rmsnorm_kernel(x_ref, g_ref, o_ref):
    o_ref[...] = _rms(x_ref[...], g_ref[...])


def rmsnorm(x, g, *, tm):
    n, d = x.shape
    assert n % tm == 0
    return pl.pallas_call(
        _rmsnorm_kernel,
        out_shape=jax.ShapeDtypeStruct((n, d), F32),
        grid=(n // tm,),
        in_specs=[pl.BlockSpec((tm, d), lambda i: (i, 0)), pl.BlockSpec((1, d), lambda i: (0, 0))],
        out_specs=pl.BlockSpec((tm, d), lambda i: (i, 0)),
        compiler_params=_params("parallel"),
        name="rmsnorm",
    )(x, g.reshape(1, d))


def _gdn_kernel(qkv_ref, z_ref, ba_ref, cprev_ref, s0_ref, cw_ref, par_ref, ng_ref,
                og_ref, s_ref, xbuf_ref, *, chunk, heads, dk, dv, t_valid):
    c = pl.program_id(1)
    hk = heads * dk
    hist = CONV_PAD_ROWS
    conv_w = cw_ref.shape[0]

    @pl.when(c == 0)
    def _():
        xbuf_ref[0:hist, :] = cprev_ref[0]
        s_ref[...] = s0_ref[...]

    xbuf_ref[hist:hist + chunk, :] = qkv_ref[0]

    row = lax.broadcasted_iota(jnp.int32, (chunk, LANES), 0)
    valid = (row + c * chunk) < t_valid
    ba = ba_ref[0]
    neg_a = -jnp.exp(par_ref[0:1, :])
    dt_bias = par_ref[1:2, :]
    beta_all = jnp.where(valid, jax.nn.sigmoid(ba), 0.0)
    g_all = jnp.where(valid, neg_a * _softplus(ba + dt_bias), 0.0)

    r_i = lax.broadcasted_iota(jnp.int32, (chunk, chunk), 0)
    c_i = lax.broadcasted_iota(jnp.int32, (chunk, chunk), 1)
    incl = c_i <= r_i
    strict = c_i < r_i
    gcum_all = jnp.dot(incl.astype(F32), g_all, precision=HIGHEST, preferred_element_type=F32)

    def conv_silu(col, width):
        acc = None
        for i in range(conv_w):
            lo = hist - (conv_w - 1) + i
            term = xbuf_ref[lo:lo + chunk, col:col + width] * cw_ref[i:i + 1, col:col + width]
            acc = term if acc is None else acc + term
        return _silu(acc)

    def l2n(x):
        return x * lax.rsqrt(jnp.sum(x * x, axis=-1, keepdims=True) + L2_EPS)

    gpad =jnp.concatenate([gcum_all, jnp.zeros((LANES - chunk, LANES), F32)], axis=0) if chunk < LANES else gcum_all
    gcum_t = gpad.T

    hs = range(heads)
    q = [l2n(conv_silu(h * dk, dk)) * (dk ** -0.5) for h in hs]
    k = [l2n(conv_silu(hk + h * dk, dk)) for h in hs]
    v = [conv_silu(2 * hk + h * dv, dv) for h in hs]
    xbuf_ref[0:hist, :] = xbuf_ref[chunk:chunk + hist, :]
    q_bf = [x.astype(BF16) for x in q]
    k_bf = [x.astype(BF16) for x in k]
    beta = [beta_all[:, h:h + 1] for h in hs]
    gc = [gcum_all[:, heads + h:heads + h + 1] for h in hs]
    decay = [jnp.exp(jnp.where(incl, gc[h] - gcum_t[heads + h:heads + h + 1, :chunk], NEG_INF)) for h in hs]
    egc = [jnp.exp(x) for x in gc]
    s = [s_ref[0, h] for h in hs]
    s_bf = [x.astype(BF16) for x in s]
    kk = [_dot_nt(k_bf[h], k_bf[h]) for h in hs]
    k_s = [_dot(k_bf[h], s_bf[h]) for h in hs]
    q_s = [_dot(q_bf[h], s_bf[h]) for h in hs]
    qk = [(_dot_nt(q_bf[h], k_bf[h]) * decay[h]).astype(BF16) for h in hs]
    power = [jnp.where(strict, beta[h] * decay[h] * kk[h], 0.0).astype(BF16) for h in hs]
    u = [beta[h] * (v[h] - egc[h] * k_s[h]) for h in hs]
    u = [u[h] - _dot(power[h], u[h].astype(BF16)) for h in hs]
    span = 2
    while span < chunk:
        power = [_dot(p, p).astype(BF16) for p in power]
        u = [u[h] + _dot(power[h], u[h].astype(BF16)) for h in hs]
        span *= 2
    u_bf = [x.astype(BF16) for x in u]
    o = [egc[h] * q_s[h] + _dot(qk[h], u_bf[h]) for h in hs]
    glast = [x[chunk - 1:chunk, :] for x in gc]
    kd = [(k[h] * jnp.exp(glast[h] - gc[h])).astype(BF16) for h in hs]
    for h in hs:
        s_ref[0, h] = jnp.exp(glast[h]) * s[h] + _dot_tn(kd[h], u_bf[h])
    for h in hs:
        zg = _silu(z_ref[0, :, h * dv:(h + 1) * dv])
        on = o[h] * lax.rsqrt(jnp.mean(o[h] * o[h], axis=-1, keepdims=True) + RMS_EPS)
        og_ref[0, :, h * dv:(h + 1) * dv] = on * ng_ref[...] * zg


def gdn_core(proj, conv_prev, s0, conv_w, a_log, dt_bias, norm_g, *, t_valid):
    bsz, tp, _ = proj.shape
    heads = a_log.shape[0]
    dv = norm_g.shape[0]
    qkv_dim = conv_w.shape[1]
    dk = (qkv_dim // heads - dv) // 2
    z_dim = heads * dv
    chunk = GDN_CHUNK
    assert tp % chunk == 0 and dk == LANES and dv == LANES and 2 * heads <= LANES
    assert conv_w.shape[0] - 1 <= CONV_PAD_ROWS <= chunk
    par = jnp.zeros((8, LANES), F32)
    par = par.at[0, heads:2 * heads].set(a_log).at[1, heads:2 * heads].set(dt_bias)
    cprev = jnp.pad(conv_prev, ((0, 0), (CONV_PAD_ROWS - conv_prev.shape[1], 0), (0, 0)))
    nq = qkv_dim // LANES
    kern = functools.partial(_gdn_kernel, chunk=chunk, heads=heads, dk=dk, dv=dv, t_valid=t_valid)
    return pl.pallas_call(
        kern,
        out_shape=(jax.ShapeDtypeStruct((bsz, tp, z_dim), F32),
                   jax.ShapeDtypeStruct((bsz, heads, dk, dv), F32)),
        grid=(bsz, tp // chunk),
        in_specs=[
            pl.BlockSpec((1, chunk, qkv_dim), lambda b, c: (b, c, 0)),
            pl.BlockSpec((1, chunk, z_dim), lambda b, c: (b, c, qkv_dim // z_dim)),
            pl.BlockSpec((1, chunk, LANES), lambda b, c: (b, c, nq + z_dim // LANES)),
            pl.BlockSpec((1, CONV_PAD_ROWS, qkv_dim), lambda b, c: (b, 0, 0)),
            pl.BlockSpec((1, heads, dk, dv), lambda b, c: (b, 0, 0, 0)),
            pl.BlockSpec(conv_w.shape, lambda b, c: (0, 0)),
            pl.BlockSpec((8, LANES), lambda b, c: (0, 0)),
            pl.BlockSpec((1, dv), lambda b, c: (0, 0)),
        ],
        out_specs=(pl.BlockSpec((1, chunk, z_dim), lambda b, c: (b, c, 0)),
                   pl.BlockSpec((1, heads, dk, dv), lambda b, c: (b, 0, 0, 0))),
        scratch_shapes=[pltpu.VMEM((CONV_PAD_ROWS + chunk, qkv_dim), F32)],
        compiler_params=_params("parallel", "arbitrary"),
        name="gdn_core",
    )(proj, proj, proj, cprev, s0, conv_w, par, norm_g.reshape(1, dv))


def _moe_kernel(x_ref, g_ref, wr_ref, br_ref, wg_ref, wu_ref, wd_ref, o_ref, xn_ref, comb_ref,
                *, n_groups, n_experts):
    e = pl.program_id(1)
    tm = x_ref.shape[0]
    epg = n_experts // n_groups
    lane = lax.broadcasted_iota(jnp.int32, (tm, LANES), 1)

    @pl.when(e == 0)
    def _():
        x = x_ref[...]
        xn = _rms(x, g_ref[...])
        xn_ref[...] = xn.astype(BF16)
        logits = jnp.dot(xn, wr_ref[...], precision=HIGHEST, preferred_element_type=F32) + br_ref[...]
        is_g = lane < n_groups
        gl = jnp.where(is_g, logits, NEG_INF)
        ge = jnp.exp(gl - jnp.max(gl, axis=-1, keepdims=True))
        pg = ge / jnp.sum(ge, axis=-1, keepdims=True)
        pg_sel = jnp.max(pg, axis=-1, keepdims=True)
        g_idx = jnp.min(jnp.where(is_g & (pg == pg_sel), lane, LANES), axis=-1, keepdims=True)
        lo = n_groups + g_idx * epg
        in_g = (lane >= lo) & (lane < lo + epg)
        el = jnp.where(in_g, logits, NEG_INF)
        ee = jnp.exp(el - jnp.max(el, axis=-1, keepdims=True))
        pe = ee / jnp.sum(ee, axis=-1, keepdims=True)
        comb = jnp.zeros((tm, LANES), F32)
        tops = []
        rest = jnp.where(in_g, pe, NEG_INF)
        for _ in range(TOPK_IN_GROUP):
            top_v = jnp.max(rest, axis=-1, keepdims=True)
            top_i = jnp.min(jnp.where(rest == top_v, lane, LANES), axis=-1, keepdims=True)
            tops.append((top_v, top_i))
            rest = jnp.where(lane == top_i, NEG_INF, rest)
        denom = tops[0][0]
        for top_v, _ in tops[1:]:
            denom = denom + top_v
        for top_v, top_i in tops:
            comb = comb + jnp.where(lane == top_i, pg_sel * top_v / denom, 0.0)
        comb_ref[...] = comb
        o_ref[...] = x

    xn = xn_ref[...]
    he = _silu(_dot(xn, wg_ref[0])) * _dot(xn, wu_ref[0])
    y = _dot(he.astype(BF16), wd_ref[0])
    ce = jnp.sum(jnp.where(lane == n_groups + e, comb_ref[...], 0.0), axis=-1, keepdims=True)
    o_ref[...] += ce * y


def hier_moe_residual(x, g, w_route, b_route, w_gate, w_up, w_down, *, n_groups, tm):
    n, d = x.shape
    n_experts, _, ff = w_gate.shape
    assert n % tm == 0 and n_groups + n_experts <= LANES
    kern = functools.partial(_moe_kernel, n_groups=n_groups, n_experts=n_experts)
    return pl.pallas_call(
        kern,
        out_shape=jax.ShapeDtypeStruct((n, d), F32),
        grid=(n // tm, n_experts),
        in_specs=[
            pl.BlockSpec((tm, d), lambda i, e: (i, 0)),
            pl.BlockSpec((1, d), lambda i, e: (0, 0)),
            pl.BlockSpec((d, LANES), lambda i, e: (0, 0)),
            pl.BlockSpec((1, LANES), lambda i, e: (0, 0)),
            pl.BlockSpec((1, d, ff), lambda i, e: (e, 0, 0)),
            pl.BlockSpec((1, d, ff), lambda i, e: (e, 0, 0)),
            pl.BlockSpec((1, ff, d), lambda i, e: (e, 0, 0)),
        ],
        out_specs=pl.BlockSpec((tm, d), lambda i, e: (i, 0)),
        scratch_shapes=[pltpu.VMEM((tm, d), BF16), pltpu.VMEM((tm, LANES), F32)],
        compiler_params=_params("parallel", "arbitrary"),
        name="hier_moe",
    )(x, g.reshape(1, d), w_route, b_route, w_gate, w_up, w_down)


def _moba_prompt_kernel(q_ref, k_ref, v_ref, o_ref, kbf_ref, vt_ref, kmean_ref, sel_ref, lg_ref,
                        *, n_blocks, n_sel, scale):
    qi = pl.program_id(2)
    blk = MOBA_BLOCK
    hd = q_ref.shape[2]

    @pl.when(qi == 0)
    def _():
        kmean_ref[...] = jnp.zeros_like(kmean_ref)
        for j in range(n_blocks):
            kj = k_ref[0, j * blk:(j + 1) * blk, :]
            kmean_ref[j:j + 1, :] = jnp.mean(kj, axis=0, keepdims=True)
            kbf_ref[j] = kj.astype(BF16)
            vt_ref[j] = v_ref[0, j * blk:(j + 1) * blk, :].T.astype(BF16)

    q = q_ref[0]
    q_bf = q.astype(BF16)
    gate = _dot_nt(kmean_ref[...], q, precision=HIGHEST)
    blk_i = lax.broadcasted_iota(jnp.int32, gate.shape, 0)
    past = blk_i < qi
    for j in range(n_blocks):
        gj = gate[j:j + 1, :]
        ahead = ((gate > gj) | ((gate == gj) & (blk_i < j))) & past
        rank = jnp.sum(jnp.where(ahead, 1.0, 0.0), axis=0, keepdims=True)
        sel_ref[j:j + 1, :] = jnp.where(rank < n_sel, 1.0, 0.0)

    def past_block(j, top):
        sc = _dot_nt(kbf_ref[j], q_bf) * scale
        sc = jnp.where(sel_ref[pl.ds(j, 1), :] > 0.5, sc, NEG_INF)
        lg_ref[j] = sc
        return jnp.maximum(top, jnp.max(sc, axis=0, keepdims=True))

    top = lax.fori_loop(0, qi, past_block, jnp.full((1, blk), NEG_INF, F32))
    key_i = lax.broadcasted_iota(jnp.int32, (blk, blk), 0)
    qry_i = lax.broadcasted_iota(jnp.int32, (blk, blk), 1)
    own = jnp.where(key_i <= qry_i, _dot_nt(kbf_ref[qi], q_bf) * scale, NEG_INF)
    lg_ref[qi] = own
    top = jnp.maximum(top, jnp.max(own, axis=0, keepdims=True))

    def weigh_block(j, carry):
        den, out = carry
        p = jnp.exp(lg_ref[j] - top)
        den = den + jnp.sum(p, axis=0, keepdims=True)
        out = out + _dot(vt_ref[j], p.astype(BF16))
        return den, out

    den, out = lax.fori_loop(0, qi + 1, weigh_block, (jnp.zeros((1, blk), F32), jnp.zeros((hd, blk), F32)))
    o_ref[0] = (out / den).T


def moba_prompt(q, k, v, *, heads):
    bsz, s, hd_all = q.shape
    hd = hd_all // heads
    assert s % MOBA_BLOCK == 0 and hd == LANES
    n_blocks = s // MOBA_BLOCK
    nb_pad = -(-n_blocks // 8) * 8
    n_sel = min(MOBA_TOPK, (s - 1) // MOBA_BLOCK)
    kern = functools.partial(_moba_prompt_kernel, n_blocks=n_blocks, n_sel=n_sel, scale=hd ** -0.5)
    return pl.pallas_call(
        kern,
        out_shape=jax.ShapeDtypeStruct((bsz, s, hd_all), F32),
        grid=(bsz, heads, n_blocks),
        in_specs=[
            pl.BlockSpec((1, MOBA_BLOCK, hd), lambda b, h, i: (b, i, h)),
            pl.BlockSpec((1, s, hd), lambda b, h, i: (b, 0, h)),
            pl.BlockSpec((1, s, hd), lambda b, h, i: (b, 0, h)),
        ],
        out_specs=pl.BlockSpec((1, MOBA_BLOCK, hd), lambda b, h, i: (b, i, h)),
        scratch_shapes=[
            pltpu.VMEM((n_blocks, MOBA_BLOCK, hd), BF16),
            pltpu.VMEM((n_blocks, hd, MOBA_BLOCK), BF16),
            pltpu.VMEM((nb_pad, hd), F32),
            pltpu.VMEM((nb_pad, MOBA_BLOCK), F32),
            pltpu.VMEM((n_blocks, MOBA_BLOCK, MOBA_BLOCK), F32),
        ],
        compiler_params=_params("parallel", "parallel", "arbitrary"),
        name="moba_prompt",
    )(q, k, v)


def _block_means_kernel(pt_ref, *refs, n_pages_step, ppb, heads):
    del pt_ref
    o_ref = refs[n_pages_step]
    for n in range(n_pages_step // ppb):
        acc = None
        for p in range(ppb):
            part = jnp.sum(refs[n * ppb + p][0], axis=0)
            acc = part if acc is None else acc + part
        mean = acc / MOBA_BLOCK
        for h in range(heads):
            o_ref[0, h, n:n + 1, :] = mean[h:h + 1, :]


def paged_block_means(cache_k, page_table, *, n_full, ppb):
    n_pool, page, heads, hd = cache_k.shape
    dbsz = page_table.shape[0]
    blocks_step = 8
    assert n_full % blocks_step == 0
    n_pages_step = blocks_step * ppb
    kern = functools.partial(_block_means_kernel, n_pages_step=n_pages_step, ppb=ppb, heads=heads)

    def page_spec(i):
        return pl.BlockSpec((1, page, heads, hd), lambda b, s, pt: (pt[b, s * n_pages_step + i], 0, 0, 0))

    return pl.pallas_call(
        kern,
        out_shape=jax.ShapeDtypeStruct((dbsz, heads, n_full, hd), F32),
        grid_spec=pltpu.PrefetchScalarGridSpec(
            num_scalar_prefetch=1,
            grid=(dbsz, n_full // blocks_step),
            in_specs=[page_spec(i) for i in range(n_pages_step)],
            out_specs=pl.BlockSpec((1, heads, blocks_step, hd), lambda b, s, pt: (b, 0, s, 0)),
        ),
        compiler_params=_params("parallel", "parallel"),
        name="paged_block_means",
    )(page_table, *([cache_k] * n_pages_step))


def _sample_select_kernel(q_ref, bm_ref, pt_ref, o_ref, *, heads, n_sel, ppb):
    rows, hd = q_ref.shape[1], q_ref.shape[2]
    nb = bm_ref.shape[2]
    n_pages = pt_ref.shape[2]
    q = q_ref[0]
    row_head = lax.broadcasted_iota(jnp.int32, (rows, nb), 0) % heads
    lane = lax.broadcasted_iota(jnp.int32, (rows, nb), 1)
    lane_pt = lax.broadcasted_iota(jnp.int32, (rows, n_pages), 1)
    lane_out = lax.broadcasted_iota(jnp.int32, (rows, LANES), 1)
    gate = jnp.zeros((rows, nb), F32)
    for h in range(heads):
        gate_h = _dot_nt(q, bm_ref[0, h], precision=HIGHEST)
        gate = jnp.where(row_head == h, gate_h, gate)
    pt = pt_ref[0].astype(F32)
    out = jnp.zeros((rows, LANES), F32)
    for s in range(n_sel):
        top = jnp.max(gate, axis=-1, keepdims=True)
        idx = jnp.min(jnp.where(gate == top, lane, nb), axis=-1, keepdims=True)
        gate = jnp.where(lane == idx, NEG_INF, gate)
        for p in range(ppb):
            phys = jnp.sum(jnp.where(lane_pt == idx * ppb + p, pt, 0.0), axis=-1, keepdims=True)
            out = out + jnp.where(lane_out == s * ppb + p, phys, 0.0)
    o_ref[0] = out.astype(jnp.int32)


def sample_select(q, bmean, page_table, *, heads, n_sel, ppb):
    dbsz, rows, hd = q.shape
    nb = bmean.shape[2]
    n_pages = page_table.shape[1]
    kern = functools.partial(_sample_select_kernel, heads=heads, n_sel=n_sel, ppb=ppb)
    return pl.pallas_call(
        kern,
        out_shape=jax.ShapeDtypeStruct((dbsz, rows, LANES), jnp.int32),
        grid=(dbsz,),
        in_specs=[
            pl.BlockSpec((1, rows, hd), lambda b: (b, 0, 0)),
            pl.BlockSpec((1, heads, nb, hd), lambda b: (b, 0, 0, 0)),
            pl.BlockSpec((1, 1, n_pages), lambda b: (b, 0, 0)),
        ],
        out_specs=pl.BlockSpec((1, rows, LANES), lambda b: (b, 0, 0)),
        compiler_params=_params("parallel"),
        name="sample_select",
    )(q, bmean, page_table.reshape(dbsz, 1, n_pages))


def _sample_attend_kernel(phys_ref, q_ref, kn_ref, vn_ref, ck_ref, cv_ref, o_ref, kbuf_ref, vbuf_ref, sem_ref,
                          *, n_slabs, t_len, heads, scale):
    n_gather = t_len * n_slabs
    step = pl.program_id(0)
    n_steps = pl.num_programs(0)
    slot = step % 2

    def slab_copies(of_step, to_slot):
        b, h = of_step // heads, of_step % heads
        copies = []
        for t in range(t_len):
            for s in range(n_slabs):
                g = t * n_slabs + s
                phys = phys_ref[((b * t_len + t) * heads + h) * n_slabs + s]
                copies.append(pltpu.make_async_copy(ck_ref.at[phys, :, h, :], kbuf_ref.at[to_slot, g],
                                                    sem_ref.at[to_slot, g]))
                copies.append(pltpu.make_async_copy(cv_ref.at[phys, :, h, :], vbuf_ref.at[to_slot, g],
                                                    sem_ref.at[to_slot, n_gather + g]))
        return copies

    @pl.when(step == 0)
    def _():
        for cp in slab_copies(step, slot):
            cp.start()

    @pl.when(step + 1 < n_steps)
    def _():
        for cp in slab_copies(step + 1, 1 - slot):
            cp.start()

    for cp in slab_copies(step, slot):
        cp.wait()
    k_refs = [kbuf_ref.at[slot, g] for g in range(n_gather)]
    v_refs = [vbuf_ref.at[slot, g] for g in range(n_gather)]

    q = q_ref[0, 0]
    rows = q.shape[0]
    q_bf = q.astype(BF16)
    k_new, v_new = kn_ref[0, 0], vn_ref[0, 0]
    row_c = lax.broadcasted_iota(jnp.int32, (rows, 1), 0)
    own = []
    for j in range(t_len):
        lg = jnp.sum(q * k_new[j:j + 1, :], axis=-1, keepdims=True) * scale
        own.append(jnp.where(row_c >= j, lg, NEG_INF))
    page = kbuf_ref.shape[2]
    row_p = lax.broadcasted_iota(jnp.int32, (rows, page), 0)
    past = []
    for s in range(n_slabs):
        lg = None
        for t in range(t_len):
            lg_t = _dot_nt(q_bf, k_refs[t * n_slabs + s][...].astype(BF16)) * scale
            lg = lg_t if lg is None else jnp.where(row_p == t, lg_t, lg)
        past.append(lg)
    top = own[0]
    for lg in own[1:]:
        top = jnp.maximum(top, lg)
    for lg in past:
        top = jnp.maximum(top, jnp.max(lg, axis=-1, keepdims=True))
    den = jnp.zeros((rows, 1), F32)
    out = jnp.zeros(q.shape, F32)
    for j in range(t_len):
        p = jnp.exp(own[j] - top)
        den = den + p
        out = out + p * v_new[j:j + 1, :]
    row_o = lax.broadcasted_iota(jnp.int32, q.shape, 0)
    for s in range(n_slabs):
        p = jnp.exp(past[s] - top)
        den = den + jnp.sum(p, axis=-1, keepdims=True)
        p_bf = p.astype(BF16)
        for t in range(t_len):
            o_t = _dot(p_bf, v_refs[t * n_slabs + s][...].astype(BF16))
            out = out + jnp.where(row_o == t, o_t, 0.0)
    o_ref[0, 0] = out / den


def sample_attend(q, k_new, v_new, cache_k, cache_v, phys, *, t_len, n_slabs):
    dbsz, heads, rows, hd = q.shape
    page = cache_k.shape[1]
    kern = functools.partial(_sample_attend_kernel, n_slabs=n_slabs, t_len=t_len, heads=heads, scale=hd ** -0.5)
    n_gather = t_len * n_slabs
    new_spec = pl.BlockSpec((1, 1, rows, hd), lambda i, ph: (i // heads, i % heads, 0, 0))
    any_spec = pl.BlockSpec(memory_space=pl.ANY)
    return pl.pallas_call(
        kern,
        out_shape=jax.ShapeDtypeStruct((dbsz, heads, rows, hd), F32),
        grid_spec=pltpu.PrefetchScalarGridSpec(
            num_scalar_prefetch=1,
            grid=(dbsz * heads,),
            in_specs=[new_spec, new_spec, new_spec, any_spec, any_spec],
            out_specs=new_spec,
            scratch_shapes=[
                pltpu.VMEM((2, n_gather, page, hd), F32),
                pltpu.VMEM((2, n_gather, page, hd), F32),
                pltpu.SemaphoreType.DMA((2, 2 * n_gather)),
            ],
        ),
        compiler_params=_params("arbitrary"),
        name="sample_attend",
    )(phys, q, k_new, v_new, cache_k, cache_v)


def _row_tile(n, target):
    tm = min(n, target)
    assert n % tm == 0
    return tm


def kernel(x_prompt, x_sample, state_gdn, state_conv, cache_k, cache_v, page_table, norm_mix, norm_ffn,
           gdn_w_in, gdn_conv_w, gdn_a_log, gdn_dt_bias, gdn_norm, gdn_w_out, kv_norm, w_kv, moba_w_q,
           moba_w_o, moe_w_group, moe_b_group, moe_w_router, moe_b_router, moe_w_gate, moe_w_up,
           moe_w_down, norm_final):
    bsz, s, d = x_prompt.shape
    dbsz, t, _ = x_sample.shape
    depth = norm_mix.shape[0]
    n_a = gdn_w_in.shape[0]
    conv_w, qkv_dim = gdn_conv_w.shape[1:]
    gdn_heads, dv = gdn_a_log.shape[1], gdn_norm.shape[1]
    z_dim = gdn_heads * dv
    n_pool, page, moba_heads, hd = cache_k.shape
    attn_dim = moba_heads * hd
    n_pages = page_table.shape[1]
    ppb = MOBA_BLOCK // page
    n_full = n_pages // ppb
    n_groups = moe_w_group.shape[-1]
    n_experts = moe_w_router.shape[-1]
    assert n_pages == n_full * ppb, "a partial past block (tail pages) is not supported"
    assert n_full >= MOBA_TOPK and t >= conv_w - 1 and s >= conv_w - 1 and t <= page

    groups = [(x_prompt.reshape(bsz * s, d), bsz, s), (x_sample.reshape(dbsz * t, d), dbsz, t)]
    tiles = [_row_tile(bsz * s, 512), _row_tile(dbsz * t, 512)]

    def moe(l, x, tm):
        w_route = jnp.zeros((d, LANES), F32).at[:, :n_groups].set(moe_w_group[l])
        w_route = w_route.at[:, n_groups:n_groups + n_experts].set(moe_w_router[l])
        b_route = jnp.zeros((1, LANES), F32).at[0, :n_groups].set(moe_b_group[l])
        b_route = b_route.at[0, n_groups:n_groups + n_experts].set(moe_b_router[l])
        return hier_moe_residual(x, norm_ffn[l], w_route, b_route, moe_w_gate[l].astype(BF16),
                                 moe_w_up[l].astype(BF16), moe_w_down[l].astype(BF16),
                                 n_groups=n_groups, tm=tm)

    xs = [g[0] for g in groups]
    gdn_states, conv_states, kvs = [[], []], [[], []], [None, None]
    for l in range(depth):
        if l < n_a:
            in_dim = gdn_w_in.shape[2]
            in_pad = -(-in_dim // (3 * LANES)) * (3 * LANES)
            assert qkv_dim + z_dim + LANES <= in_pad and in_dim - qkv_dim - z_dim == 2 * gdn_heads
            w_in = jnp.pad(gdn_w_in[l], ((0, 0), (0, in_pad - in_dim))).astype(BF16)
            w_out = gdn_w_out[l].astype(BF16)
            for gi, (_, nb, tl) in enumerate(groups):
                tm = tiles[gi]
                proj = norm_matmul(xs[gi], norm_mix[l], w_in, tm=tm, tn=3 * LANES).reshape(nb, tl, in_pad)
                tp = -(-tl // GDN_CHUNK) * GDN_CHUNK
                proj_p = jnp.pad(proj, ((0, 0), (0, tp - tl), (0, 0))) if tp != tl else proj
                if gi == 0:
                    conv0 = jnp.zeros((nb, conv_w - 1, qkv_dim), F32)
                    s0 = jnp.zeros((nb, gdn_heads, (qkv_dim // gdn_heads - dv) // 2, dv), F32)
                else:
                    conv0, s0 = state_conv[l], state_gdn[l]
                og, s_fin = gdn_core(proj_p, conv0, s0, gdn_conv_w[l], gdn_a_log[l], gdn_dt_bias[l],
                                     gdn_norm[l], t_valid=tl)
                og = og[:, :tl].reshape(nb * tl, z_dim)
                gdn_states[gi].append(s_fin)
                conv_states[gi].append(proj[:, tl - (conv_w - 1):, :qkv_dim])
                xs[gi] = matmul_residual(og, w_out, xs[gi], tm=tm, tn=512)
        else:
            j = l - n_a
            w_q = moba_w_q[j].astype(BF16)
            w_o = moba_w_o[j].astype(BF16)
            q = norm_matmul(xs[0], norm_mix[l], w_q, tm=tiles[0], tn=512).reshape(bsz, s, attn_dim)
            k_p, v_p = kvs[0]
            o = moba_prompt(q, k_p.reshape(bsz, s, attn_dim), v_p.reshape(bsz, s, attn_dim), heads=moba_heads)
            xs[0] = matmul_residual(o.reshape(bsz * s, attn_dim), w_o, xs[0], tm=tiles[0], tn=512)
            q = norm_matmul(xs[1], norm_mix[l], w_q, tm=tiles[1], tn=512)
            k_s, v_s = kvs[1]
            n_sel = min(MOBA_TOPK, n_full)
            phys = sample_select(q.reshape(dbsz, t * moba_heads, hd), bmean, page_table,
                                 heads=moba_heads, n_sel=n_sel, ppb=ppb)
            phys = phys[:, :, :n_sel * ppb].reshape(-1)

            def head_major(a):
                a = a.reshape(dbsz, t, moba_heads, hd).transpose(0, 2, 1, 3)
                return jnp.pad(a, ((0, 0), (0, 0), (0, -t % 8), (0, 0)))

            o = sample_attend(head_major(q), head_major(k_s), head_major(v_s), cache_k, cache_v, phys,
                              t_len=t, n_slabs=n_sel * ppb)
            o = o[:, :, :t].transpose(0, 2, 1, 3).reshape(dbsz * t, attn_dim)
            xs[1] = matmul_residual(o, w_o, xs[1], tm=tiles[1], tn=512)
        for gi in range(2):
            xs[gi] = moe(l, xs[gi], tiles[gi])
        if l == n_a - 1:
            w_kv_bf = w_kv.astype(BF16)
            for gi, (_, nb, tl) in enumerate(groups):
                kv = norm_matmul(xs[gi], kv_norm, w_kv_bf, tm=tiles[gi], tn=512)
                kvs[gi] = (kv[:, :attn_dim].reshape(nb, tl, moba_heads, hd),
                           kv[:, attn_dim:].reshape(nb, tl, moba_heads, hd))
            bmean = paged_block_means(cache_k, page_table, n_full=n_full, ppb=ppb)
    y_p = rmsnorm(xs[0], norm_final, tm=tiles[0]).reshape(bsz, s, d)
    y_s = rmsnorm(xs[1], norm_final, tm=tiles[1]).reshape(dbsz, t, d)
    return (y_p, y_s, jnp.stack(gdn_states[0]), jnp.stack(conv_states[0]), kvs[0][0], kvs[0][1],
            jnp.stack(gdn_states[1]), jnp.stack(conv_states[1]), kvs[1][0], kvs[1][1])
```

```python
import functools

import jax
import jax.numpy as jnp
from jax import lax
from jax.experimental import pallas as pl
from jax.experimental.pallas import tpu as pltpu

RMS_EPS = 1e-6
L2_EPS = 1e-6
MOBA_BLOCK = 256
MOBA_TOPK = 3
TOPK_IN_GROUP = 2
GDN_CHUNK = 64
LANES = 128
SUBLANES = 8
ROW_TILE = 1024
COL_TILE = 1024
GDN_PROJ_COL_TILE = 11 * LANES
MOE_TOKEN_TILE = 2048
MOE_ROW_BLOCK = 256
CONV_PAD_ROWS = SUBLANES
VMEM_LIMIT = 56 * 1024 * 1024

F32 = jnp.float32
BF16 = jnp.bfloat16
HIGHEST = lax.Precision.HIGHEST
NEG_INF = float("-inf")


def _params(*sem):
    return pltpu.CompilerParams(dimension_semantics=sem, vmem_limit_bytes=VMEM_LIMIT)


def _rms(x, g):
    return x * lax.rsqrt(jnp.mean(x * x, axis=-1, keepdims=True) + RMS_EPS) * g


def _silu(x):
    return x * jax.nn.sigmoid(x)


def _softplus(x):
    return jnp.maximum(x, 0.0) + jnp.log1p(jnp.exp(-jnp.abs(x)))


def _dot(a, b):
    return jnp.dot(a, b, preferred_element_type=F32)


def _dot_nt(a, b, precision=None):
    return lax.dot_general(a, b, (((1,), (1,)), ((), ())), precision=precision, preferred_element_type=F32)


def _dot_tn(a, b):
    return lax.dot_general(a, b, (((0,), (0,)), ((), ())), preferred_element_type=F32)


def _norm_matmul_kernel(x_ref, g_ref, w_ref, o_ref, xn_ref):
    @pl.when(pl.program_id(1) == 0)
    def _():
        xn_ref[...] = _rms(x_ref[...], g_ref[...]).astype(BF16)

    o_ref[...] = _dot(xn_ref[...], w_ref[...])


def norm_matmul(x, g, w, *, tm, tn):
    n, d = x.shape
    m = w.shape[1]
    assert n % tm == 0 and m % tn == 0
    return pl.pallas_call(
        _norm_matmul_kernel,
        out_shape=jax.ShapeDtypeStruct((n, m), F32),
        grid=(n // tm, m // tn),
        in_specs=[
            pl.BlockSpec((tm, d), lambda i, j: (i, 0)),
            pl.BlockSpec((1, d), lambda i, j: (0, 0)),
            pl.BlockSpec((d, tn), lambda i, j: (0, j)),
        ],
        out_specs=pl.BlockSpec((tm, tn), lambda i, j: (i, j)),
        scratch_shapes=[pltpu.VMEM((tm, d), BF16)],
        compiler_params=_params("parallel", "arbitrary"),
        name="norm_matmul",
    )(x, g.reshape(1, d), w)


def _matmul_residual_kernel(a_ref, w_ref, r_ref, o_ref):
    o_ref[...] = r_ref[...] + _dot(a_ref[...].astype(BF16), w_ref[...])


def matmul_residual(a, w, resid, *, tm, tn):
    n, k = a.shape
    m = w.shape[1]
    assert n % tm == 0 and m % tn == 0
    return pl.pallas_call(
        _matmul_residual_kernel,
        out_shape=jax.ShapeDtypeStruct((n, m), F32),
        grid=(n // tm, m // tn),
        in_specs=[
            pl.BlockSpec((tm, k), lambda i, j: (i, 0)),
            pl.BlockSpec((k, tn), lambda i, j: (0, j)),
            pl.BlockSpec((tm, tn), lambda i, j: (i, j)),
        ],
        out_specs=pl.BlockSpec((tm, tn), lambda i, j: (i, j)),
        compiler_params=_params("parallel", "parallel"),
        name="matmul_residual",
    )(a, w, resid)


def _rmsnorm_kernel(x_ref, g_ref, o_ref):
    o_ref[...] = _rms(x_ref[...], g_ref[...])


def rmsnorm(x, g, *, tm):
    n, d = x.shape
    assert n % tm == 0
    return pl.pallas_call(
        _rmsnorm_kernel,
        out_shape=jax.ShapeDtypeStruct((n, d), F32),
        grid=(n // tm,),
        in_specs=[pl.BlockSpec((tm, d), lambda i: (i, 0)), pl.BlockSpec((1, d), lambda i: (0, 0))],
        out_specs=pl.BlockSpec((tm, d), lambda i: (i, 0)),
        compiler_params=_params("parallel"),
        name="rmsnorm",
    )(x, g.reshape(1, d))


def _gdn_kernel(qkv_ref, z_ref, ba_ref, cprev_ref, s0_ref, cw_ref, par_ref, ng_ref,
                og_ref, s_ref, xbuf_ref, *, chunk, heads, dk, dv, t_valid):
    c = pl.program_id(1)
    hk = heads * dk
    hist = CONV_PAD_ROWS
    conv_w = cw_ref.shape[0]

    @pl.when(c == 0)
    def _():
        xbuf_ref[0:hist, :] = cprev_ref[0]
        s_ref[...] = s0_ref[...]

    xbuf_ref[hist:hist + chunk, :] = qkv_ref[0]

    row = lax.broadcasted_iota(jnp.int32, (chunk, LANES), 0)
    valid = (row + c * chunk) < t_valid
    ba = ba_ref[0]
    neg_a = -jnp.exp(par_ref[0:1, :])
    dt_bias = par_ref[1:2, :]
    beta_all = jnp.where(valid, jax.nn.sigmoid(ba), 0.0)
    g_all = jnp.where(valid, neg_a * _softplus(ba + dt_bias), 0.0)

    r_i = lax.broadcasted_iota(jnp.int32, (chunk, chunk), 0)
    c_i = lax.broadcasted_iota(jnp.int32, (chunk, chunk), 1)
    incl = c_i <= r_i
    strict = c_i < r_i
    gcum_all = jnp.dot(incl.astype(F32), g_all, precision=HIGHEST, preferred_element_type=F32)

    def conv_silu(col, width):
        acc = None
        for i in range(conv_w):
            lo = hist - (conv_w - 1) + i
            term = xbuf_ref[lo:lo + chunk, col:col + width] * cw_ref[i:i + 1, col:col + width]
            acc = term if acc is None else acc + term
        return _silu(acc)

    def l2n(x):
        return x * lax.rsqrt(jnp.sum(x * x, axis=-1, keepdims=True) + L2_EPS)

    gpad =jnp.concatenate([gcum_all, jnp.zeros((LANES - chunk, LANES), F32)], axis=0) if chunk < LANES else gcum_all
    gcum_t = gpad.T

    hs = range(heads)
    q = [l2n(conv_silu(h * dk, dk)) * (dk ** -0.5) for h in hs]
    k = [l2n(conv_silu(hk + h * dk, dk)) for h in hs]
    v = [conv_silu(2 * hk + h * dv, dv) for h in hs]
    xbuf_ref[0:hist, :] = xbuf_ref[chunk:chunk + hist, :]
    q_bf = [x.astype(BF16) for x in q]
    k_bf = [x.astype(BF16) for x in k]
    beta = [beta_all[:, h:h + 1] for h in hs]
    gc = [gcum_all[:, heads + h:heads + h + 1] for h in hs]
    decay = [jnp.exp(jnp.where(incl, gc[h] - gcum_t[heads + h:heads + h + 1, :chunk], NEG_INF)) for h in hs]
    egc = [jnp.exp(x) for x in gc]
    s = [s_ref[0, h] for h in hs]
    s_bf = [x.astype(BF16) for x in s]
    kk = [_dot_nt(k_bf[h], k_bf[h]) for h in hs]
    k_s = [_dot(k_bf[h], s_bf[h]) for h in hs]
    q_s = [_dot(q_bf[h], s_bf[h]) for h in hs]
    qk = [(_dot_nt(q_bf[h], k_bf[h]) * decay[h]).astype(BF16) for h in hs]
    power = [jnp.where(strict, beta[h] * decay[h] * kk[h], 0.0).astype(BF16) for h in hs]
    u = [beta[h] * (v[h] - egc[h] * k_s[h]) for h in hs]
    powers = [power]
    span = 2
    while span < chunk:
        powers.append([_dot(p, p).astype(BF16) for p in powers[-1]])
        span *= 2
    for power in reversed(powers[1:]):
        u = [u[h] + _dot(power[h], u[h].astype(BF16)) for h in hs]
    u = [u[h] - _dot(powers[0][h], u[h].astype(BF16)) for h in hs]
    u_bf = [x.astype(BF16) for x in u]
    o = [egc[h] * q_s[h] + _dot(qk[h], u_bf[h]) for h in hs]
    glast = [x[chunk - 1:chunk, :] for x in gc]
    kd = [(k[h] * jnp.exp(glast[h] - gc[h])).astype(BF16) for h in hs]
    for h in hs:
        s_ref[0, h] = jnp.exp(glast[h]) * s[h] + _dot_tn(kd[h], u_bf[h])
    for h in hs:
        zg = _silu(z_ref[0, :, h * dv:(h + 1) * dv])
        on = o[h] * lax.rsqrt(jnp.mean(o[h] * o[h], axis=-1, keepdims=True) + RMS_EPS)
        og_ref[0, :, h * dv:(h + 1) * dv] = on * ng_ref[...] * zg


def gdn_core(proj, conv_prev, s0, conv_w, a_log, dt_bias, norm_g, *, t_valid):
    bsz, tp, _ = proj.shape
    heads = a_log.shape[0]
    dv = norm_g.shape[0]
    qkv_dim = conv_w.shape[1]
    dk = (qkv_dim // heads - dv) // 2
    z_dim = heads * dv
    chunk = GDN_CHUNK
    assert tp % chunk == 0 and dk == LANES and dv == LANES and 2 * heads <= LANES
    assert conv_w.shape[0] - 1 <= CONV_PAD_ROWS <= chunk
    par = jnp.zeros((8, LANES), F32)
    par = par.at[0, heads:2 * heads].set(a_log).at[1, heads:2 * heads].set(dt_bias)
    cprev = jnp.pad(conv_prev, ((0, 0), (CONV_PAD_ROWS - conv_prev.shape[1], 0), (0, 0)))
    nq = qkv_dim // LANES
    kern = functools.partial(_gdn_kernel, chunk=chunk, heads=heads, dk=dk, dv=dv, t_valid=t_valid)
    return pl.pallas_call(
        kern,
        out_shape=(jax.ShapeDtypeStruct((bsz, tp, z_dim), F32),
                   jax.ShapeDtypeStruct((bsz, heads, dk, dv), F32)),
        grid=(bsz, tp // chunk),
        in_specs=[
            pl.BlockSpec((1, chunk, qkv_dim), lambda b, c: (b, c, 0)),
            pl.BlockSpec((1, chunk, z_dim), lambda b, c: (b, c, qkv_dim // z_dim)),
            pl.BlockSpec((1, chunk, LANES), lambda b, c: (b, c, nq + z_dim // LANES)),
            pl.BlockSpec((1, CONV_PAD_ROWS, qkv_dim), lambda b, c: (b, 0, 0)),
            pl.BlockSpec((1, heads, dk, dv), lambda b, c: (b, 0, 0, 0)),
            pl.BlockSpec(conv_w.shape, lambda b, c: (0, 0)),
            pl.BlockSpec((8, LANES), lambda b, c: (0, 0)),
            pl.BlockSpec((1, dv), lambda b, c: (0, 0)),
        ],
        out_specs=(pl.BlockSpec((1, chunk, z_dim), lambda b, c: (b, c, 0)),
                   pl.BlockSpec((1, heads, dk, dv), lambda b, c: (b, 0, 0, 0))),
        scratch_shapes=[pltpu.VMEM((CONV_PAD_ROWS + chunk, qkv_dim), F32)],
        compiler_params=_params("parallel", "arbitrary"),
        name="gdn_core",
    )(proj, proj, proj, cprev, s0, conv_w, par, norm_g.reshape(1, dv))


def _moe_route_kernel(x_ref, g_ref, wr_ref, br_ref, pos_ref, wts_ref, seg_ref, *, n_groups, n_experts):
    tm = x_ref.shape[0]
    epg = n_experts // n_groups
    xn = _rms(x_ref[...], g_ref[...])
    logits = jnp.dot(xn, wr_ref[...], precision=HIGHEST, preferred_element_type=F32) + br_ref[...]
    lt = logits.T
    sub = lax.broadcasted_iota(jnp.int32, lt.shape, 0)
    is_g = sub < n_groups
    gl = jnp.where(is_g, lt, NEG_INF)
    ge = jnp.exp(gl - jnp.max(gl, axis=0, keepdims=True))
    pg = ge / jnp.sum(ge, axis=0, keepdims=True)
    pg_sel = jnp.max(pg, axis=0, keepdims=True)
    g_idx = jnp.min(jnp.where(is_g & (pg == pg_sel), sub, LANES), axis=0, keepdims=True)
    lo = n_groups + g_idx * epg
    in_g = (sub >= lo) & (sub < lo + epg)
    el = jnp.where(in_g, lt, NEG_INF)
    ee = jnp.exp(el - jnp.max(el, axis=0, keepdims=True))
    pe = ee / jnp.sum(ee, axis=0, keepdims=True)
    rest = jnp.where(in_g, pe, NEG_INF)
    tops = []
    for _ in range(TOPK_IN_GROUP):
        top_v = jnp.max(rest, axis=0, keepdims=True)
        top_i = jnp.min(jnp.where(rest == top_v, sub, LANES), axis=0, keepdims=True)
        tops.append((top_v, top_i))
        rest = jnp.where(sub == top_i, NEG_INF, rest)
    denom = tops[0][0]
    for top_v, _ in tops[1:]:
        denom = denom + top_v
    wts_ref[0] = jnp.concatenate([pg_sel * top_v / denom for top_v, _ in tops], axis=1)

    onehot = [jnp.where(sub == top_i, 1.0, 0.0) for _, top_i in tops]
    span = min(tm, 512)
    earlier = jnp.where(lax.broadcasted_iota(jnp.int32, (span, span), 0)
                        < lax.broadcasted_iota(jnp.int32, (span, span), 1), 1.0, 0.0).astype(BF16)
    count = [jnp.sum(oh, axis=1, keepdims=True) for oh in onehot]
    total = count[0]
    for c in count[1:]:
        total = total + c
    padded = jnp.floor((total + (SUBLANES - 1)) * (1.0 / SUBLANES)) * SUBLANES
    below = jnp.where(lax.broadcasted_iota(jnp.int32, (LANES, LANES), 1) < lax.broadcasted_iota(jnp.int32, (LANES, LANES), 0),
                      1.0, 0.0)
    start = jnp.dot(below, jnp.broadcast_to(padded, (LANES, LANES)), precision=HIGHEST,
                    preferred_element_type=F32)[:, 0:1]
    pos = []
    base = start
    for oh, c in zip(onehot, count):
        for lo_n in range(0, tm, span):
            oh_n = oh[:, lo_n:lo_n + span]
            before = _dot(oh_n.astype(BF16), earlier)
            pos.append(jnp.sum(oh_n * (base + before), axis=0, keepdims=True))
            base = base + jnp.sum(oh_n, axis=1, keepdims=True)
    pos_ref[0] = jnp.concatenate(pos, axis=1).astype(jnp.int32)
    lane = lax.broadcasted_iota(jnp.int32, (LANES, LANES), 1)
    cols = jnp.where(lane == 0, start, 0.0) + jnp.where(lane == 1, total, 0.0)
    seg_ref[0] = cols.T[0:8, :].astype(jnp.int32)


def _moe_expert_kernel(pos_ref, wts_ref, seg_ref, x_ref, g_ref, wg_ref, wu_ref, wd_ref, o_ref, rows_ref,
                       *, n_groups, n_experts, n_slots, row_block):
    e = pl.program_id(1)
    tm, d = x_ref.shape

    @pl.when(e == 0)
    def _():
        o_ref[...] = _rms(x_ref[...], g_ref[...])
        end = 0
        for j in range(n_experts):
            end = seg_ref[0, 0, n_groups + j] + seg_ref[0, 1, n_groups + j]
            last_tile = pl.multiple_of(jnp.maximum((end - 1) // SUBLANES * SUBLANES, 0), SUBLANES)
            rows_ref[pl.ds(last_tile, SUBLANES), :] = jnp.zeros((SUBLANES, d), F32)
        tail = pl.multiple_of((end + SUBLANES - 1) // SUBLANES * SUBLANES, SUBLANES)
        rows_ref[pl.ds(tail, row_block), :] = jnp.zeros((row_block, d), F32)

        def scatter(n, carry):
            row = o_ref[pl.ds(n, 1), :]
            for k in range(n_slots):
                rows_ref[pl.ds(pos_ref[0, 0, k * tm + n], 1), :] = row
            return carry

        lax.fori_loop(0, tm, scatter, 0, unroll=8)

    start = seg_ref[0, 0, n_groups + e]
    count = seg_ref[0, 1, n_groups + e]

    def run_block(i, carry):
        r0 = pl.multiple_of(start + i * row_block, SUBLANES)
        xin = rows_ref[pl.ds(r0, row_block), :]
        xb = xin.astype(BF16)
        he = _silu(_dot(xb, wg_ref[0])) * _dot(xb, wu_ref[0])
        y = _dot(he.astype(BF16), wd_ref[0])
        mine = lax.broadcasted_iota(jnp.int32, (row_block, 1), 0) < count - i * row_block
        rows_ref[pl.ds(r0, row_block), :] = jnp.where(mine, y, xin)
        return carry

    lax.fori_loop(0, (count + row_block - 1) // row_block, run_block, 0)

    @pl.when(e == n_experts - 1)
    def _():
        def gather(n, carry):
            acc = x_ref[pl.ds(n, 1), :]
            for k in range(n_slots):
                acc = acc + wts_ref[0, 0, k * tm + n] * rows_ref[pl.ds(pos_ref[0, 0, k * tm + n], 1), :]
            o_ref[pl.ds(n, 1), :] = acc
            return carry

        lax.fori_loop(0, tm, gather, 0, unroll=8)


def hier_moe_residual(x, g, w_route, b_route, w_gate, w_up, w_down, *, n_groups, tm):
    n, d = x.shape
    n_experts, _, ff = w_gate.shape
    n_slots = TOPK_IN_GROUP
    assert n % tm == 0 and tm % LANES == 0 and n_groups + n_experts <= LANES
    n_tiles = n // tm
    row_block = min(MOE_ROW_BLOCK, tm)
    rows_cap = -(-(n_slots * tm + n_experts * (SUBLANES - 1)) // SUBLANES) * SUBLANES + row_block
    route =functools.partial(_moe_route_kernel, n_groups=n_groups, n_experts=n_experts)
    pos, wts, seg = pl.pallas_call(
        route,
        out_shape=(jax.ShapeDtypeStruct((n_tiles, 1, n_slots * tm), jnp.int32),
                   jax.ShapeDtypeStruct((n_tiles, 1, n_slots * tm), F32),
                   jax.ShapeDtypeStruct((n_tiles, 8, LANES), jnp.int32)),
        grid=(n_tiles,),
        in_specs=[
            pl.BlockSpec((tm, d), lambda i: (i, 0)),
            pl.BlockSpec((1, d), lambda i: (0, 0)),
            pl.BlockSpec((d, LANES), lambda i: (0, 0)),
            pl.BlockSpec((1, LANES), lambda i: (0, 0)),
        ],
        out_specs=(pl.BlockSpec((1, 1, n_slots * tm), lambda i: (i, 0, 0)),
                   pl.BlockSpec((1, 1, n_slots * tm), lambda i: (i, 0, 0)),
                   pl.BlockSpec((1, 8, LANES), lambda i: (i, 0, 0))),
        compiler_params=_params("parallel"),
        name="moe_route",
    )(x, g.reshape(1, d), w_route, b_route)

    def smem_spec(shape):
        return pl.BlockSpec((1,) + shape, lambda i, e: (i, 0, 0), memory_space=pltpu.SMEM)

    expert = functools.partial(_moe_expert_kernel, n_groups=n_groups, n_experts=n_experts, n_slots=n_slots,
                               row_block=row_block)
    once = pl.Buffered(1)
    return pl.pallas_call(
        expert,
        out_shape=jax.ShapeDtypeStruct((n, d), F32),
        grid=(n_tiles, n_experts),
        in_specs=[
            smem_spec((1, n_slots * tm)),
            smem_spec((1, n_slots * tm)),
            smem_spec((8, LANES)),
            pl.BlockSpec((tm, d), lambda i, e: (i, 0), pipeline_mode=once),
            pl.BlockSpec((1, d), lambda i, e: (0, 0)),
            pl.BlockSpec((1, d, ff), lambda i, e: (e, 0, 0)),
            pl.BlockSpec((1, d, ff), lambda i, e: (e, 0, 0)),
            pl.BlockSpec((1, ff, d), lambda i, e: (e, 0, 0)),
        ],
        out_specs=pl.BlockSpec((tm, d), lambda i, e: (i, 0), pipeline_mode=once),
        scratch_shapes=[pltpu.VMEM((rows_cap, d), F32)],
        compiler_params=_params("parallel", "arbitrary"),
        name="moe_experts",
    )(pos, wts, seg, x, g.reshape(1, d), w_gate, w_up, w_down)


def _moba_prompt_kernel(q_ref, k_ref, v_ref, o_ref, kbf_ref, vt_ref, kmean_ref, sel_ref, lg_ref,
                        *, n_blocks, n_sel, scale):
    qi = pl.program_id(2)
    blk = MOBA_BLOCK
    hd = q_ref.shape[2]

    @pl.when(qi == 0)
    def _():
        kmean_ref[...] = jnp.zeros_like(kmean_ref)
        for j in range(n_blocks):
            kj = k_ref[0, j * blk:(j + 1) * blk, :]
            kmean_ref[j:j + 1, :] = jnp.mean(kj, axis=0, keepdims=True)
            kbf_ref[j] = kj.astype(BF16)
            vt_ref[j] = v_ref[0, j * blk:(j + 1) * blk, :].T.astype(BF16)

    q = q_ref[0]
    q_bf = q.astype(BF16)
    gate = _dot_nt(kmean_ref[...], q, precision=HIGHEST)
    blk_i = lax.broadcasted_iota(jnp.int32, gate.shape, 0)
    past = blk_i < qi
    for j in range(n_blocks):
        gj = gate[j:j + 1, :]
        ahead = ((gate > gj) | ((gate == gj) & (blk_i < j))) & past
        rank = jnp.sum(jnp.where(ahead, 1.0, 0.0), axis=0, keepdims=True)
        sel_ref[j:j + 1, :] = jnp.where(rank < n_sel, 1.0, 0.0)

    def past_block(j, top):
        sc = _dot_nt(kbf_ref[j], q_bf) * scale
        sc = jnp.where(sel_ref[pl.ds(j, 1), :] > 0.5, sc, NEG_INF)
        lg_ref[j] = sc
        return jnp.maximum(top, jnp.max(sc, axis=0, keepdims=True))

    top = lax.fori_loop(0, qi, past_block, jnp.full((1, blk), NEG_INF, F32))
    key_i = lax.broadcasted_iota(jnp.int32, (blk, blk), 0)
    qry_i = lax.broadcasted_iota(jnp.int32, (blk, blk), 1)
    own = jnp.where(key_i <= qry_i, _dot_nt(kbf_ref[qi], q_bf) * scale, NEG_INF)
    lg_ref[qi] = own
    top = jnp.maximum(top, jnp.max(own, axis=0, keepdims=True))

    def weigh_block(j, carry):
        den, out = carry
        p = jnp.exp(lg_ref[j] - top)
        den = den + jnp.sum(p, axis=0, keepdims=True)
        out = out + _dot(vt_ref[j], p.astype(BF16))
        return den, out

    den, out = lax.fori_loop(0, qi + 1, weigh_block, (jnp.zeros((1, blk), F32), jnp.zeros((hd, blk), F32)))
    o_ref[0] = (out / den).T


def moba_prompt(q, k, v, *, heads):
    bsz, s, hd_all = q.shape
    hd = hd_all // heads
    assert s % MOBA_BLOCK == 0 and hd == LANES
    n_blocks = s // MOBA_BLOCK
    nb_pad = -(-n_blocks // 8) * 8
    n_sel = min(MOBA_TOPK, (s - 1) // MOBA_BLOCK)
    kern = functools.partial(_moba_prompt_kernel, n_blocks=n_blocks, n_sel=n_sel, scale=hd ** -0.5)
    return pl.pallas_call(
        kern,
        out_shape=jax.ShapeDtypeStruct((bsz, s, hd_all), F32),
        grid=(bsz, heads, n_blocks),
        in_specs=[
            pl.BlockSpec((1, MOBA_BLOCK, hd), lambda b, h, i: (b, i, h)),
            pl.BlockSpec((1, s, hd), lambda b, h, i: (b, 0, h)),
            pl.BlockSpec((1, s, hd), lambda b, h, i: (b, 0, h)),
        ],
        out_specs=pl.BlockSpec((1, MOBA_BLOCK, hd), lambda b, h, i: (b, i, h)),
        scratch_shapes=[
            pltpu.VMEM((n_blocks, MOBA_BLOCK, hd), BF16),
            pltpu.VMEM((n_blocks, hd, MOBA_BLOCK), BF16),
            pltpu.VMEM((nb_pad, hd), F32),
            pltpu.VMEM((nb_pad, MOBA_BLOCK), F32),
            pltpu.VMEM((n_blocks, MOBA_BLOCK, MOBA_BLOCK), F32),
        ],
        compiler_params=_params("parallel", "parallel", "arbitrary"),
        name="moba_prompt",
    )(q, k, v)


def _block_means_kernel(pt_ref, *refs, n_pages_step, ppb, heads):
    del pt_ref
    o_ref = refs[n_pages_step]
    for n in range(n_pages_step // ppb):
        acc = None
        for p in range(ppb):
            part = jnp.sum(refs[n * ppb + p][0], axis=0)
            acc = part if acc is None else acc + part
        mean = acc / MOBA_BLOCK
        for h in range(heads):
            o_ref[0, h, n:n + 1, :] = mean[h:h + 1, :]


def paged_block_means(cache_k, page_table, *, n_full, ppb):
    n_pool, page, heads, hd = cache_k.shape
    dbsz = page_table.shape[0]
    blocks_step = 8
    assert n_full % blocks_step == 0
    n_pages_step = blocks_step * ppb
    kern = functools.partial(_block_means_kernel, n_pages_step=n_pages_step, ppb=ppb, heads=heads)

    def page_spec(i):
        return pl.BlockSpec((1, page, heads, hd), lambda b, s, pt: (pt[b, s * n_pages_step + i], 0, 0, 0))

    return pl.pallas_call(
        kern,
        out_shape=jax.ShapeDtypeStruct((dbsz, heads, n_full, hd), F32),
        grid_spec=pltpu.PrefetchScalarGridSpec(
            num_scalar_prefetch=1,
            grid=(dbsz, n_full // blocks_step),
            in_specs=[page_spec(i) for i in range(n_pages_step)],
            out_specs=pl.BlockSpec((1, heads, blocks_step, hd), lambda b, s, pt: (b, 0, s, 0)),
        ),
        compiler_params=_params("parallel", "parallel"),
        name="paged_block_means",
    )(page_table, *([cache_k] * n_pages_step))


def _sample_select_kernel(q_ref, bm_ref, pt_ref, o_ref, *, heads, n_sel, ppb):
    rows, hd = q_ref.shape[1], q_ref.shape[2]
    nb = bm_ref.shape[2]
    n_pages = pt_ref.shape[2]
    q = q_ref[0]
    row_head = lax.broadcasted_iota(jnp.int32, (rows, nb), 0) % heads
    lane = lax.broadcasted_iota(jnp.int32, (rows, nb), 1)
    lane_pt = lax.broadcasted_iota(jnp.int32, (rows, n_pages), 1)
    lane_out = lax.broadcasted_iota(jnp.int32, (rows, LANES), 1)
    gate = jnp.zeros((rows, nb), F32)
    for h in range(heads):
        gate_h = _dot_nt(q, bm_ref[0, h], precision=HIGHEST)
        gate = jnp.where(row_head == h, gate_h, gate)
    pt = pt_ref[0].astype(F32)
    out = jnp.zeros((rows, LANES), F32)
    for s in range(n_sel):
        top = jnp.max(gate, axis=-1, keepdims=True)
        idx = jnp.min(jnp.where(gate == top, lane, nb), axis=-1, keepdims=True)
        gate = jnp.where(lane == idx, NEG_INF, gate)
        for p in range(ppb):
            phys = jnp.sum(jnp.where(lane_pt == idx * ppb + p, pt, 0.0), axis=-1, keepdims=True)
            out = out + jnp.where(lane_out == s * ppb + p, phys, 0.0)
    o_ref[0] = out.astype(jnp.int32)


def sample_select(q, bmean, page_table, *, heads, n_sel, ppb):
    dbsz, rows, hd = q.shape
    nb = bmean.shape[2]
    n_pages = page_table.shape[1]
    kern = functools.partial(_sample_select_kernel, heads=heads, n_sel=n_sel, ppb=ppb)
    return pl.pallas_call(
        kern,
        out_shape=jax.ShapeDtypeStruct((dbsz, rows, LANES), jnp.int32),
        grid=(dbsz,),
        in_specs=[
            pl.BlockSpec((1, rows, hd), lambda b: (b, 0, 0)),
            pl.BlockSpec((1, heads, nb, hd), lambda b: (b, 0, 0, 0)),
            pl.BlockSpec((1, 1, n_pages), lambda b: (b, 0, 0)),
        ],
        out_specs=pl.BlockSpec((1, rows, LANES), lambda b: (b, 0, 0)),
        compiler_params=_params("parallel"),
        name="sample_select",
    )(q, bmean, page_table.reshape(dbsz, 1, n_pages))


def _sample_attend_kernel(phys_ref, q_ref, kn_ref, vn_ref, ck_ref, cv_ref, o_ref, kbuf_ref, vbuf_ref, sem_ref,
                          *, n_slabs, t_len, heads, scale):
    n_gather = t_len * n_slabs
    step = pl.program_id(0)
    n_steps = pl.num_programs(0)
    slot = step % 2

    def slab_copies(of_step, to_slot):
        b, h = of_step // heads, of_step % heads
        copies = []
        for t in range(t_len):
            for s in range(n_slabs):
                g = t * n_slabs + s
                phys = phys_ref[((b * t_len + t) * heads + h) * n_slabs + s]
                copies.append(pltpu.make_async_copy(ck_ref.at[phys, :, h, :], kbuf_ref.at[to_slot, g],
                                                    sem_ref.at[to_slot, g]))
                copies.append(pltpu.make_async_copy(cv_ref.at[phys, :, h, :], vbuf_ref.at[to_slot, g],
                                                    sem_ref.at[to_slot, n_gather + g]))
        return copies

    @pl.when(step == 0)
    def _():
        for cp in slab_copies(step, slot):
            cp.start()

    @pl.when(step + 1 < n_steps)
    def _():
        for cp in slab_copies(step + 1, 1 - slot):
            cp.start()

    for cp in slab_copies(step, slot):
        cp.wait()
    k_refs = [kbuf_ref.at[slot, g] for g in range(n_gather)]
    v_refs = [vbuf_ref.at[slot, g] for g in range(n_gather)]

    q = q_ref[0, 0]
    rows = q.shape[0]
    q_bf = q.astype(BF16)
    k_new, v_new = kn_ref[0, 0], vn_ref[0, 0]
    row_c = lax.broadcasted_iota(jnp.int32, (rows, 1), 0)
    own = []
    for j in range(t_len):
        lg = jnp.sum(q * k_new[j:j + 1, :], axis=-1, keepdims=True) * scale
        own.append(jnp.where(row_c >= j, lg, NEG_INF))
    page = kbuf_ref.shape[2]
    row_p = lax.broadcasted_iota(jnp.int32, (rows, page), 0)
    past = []
    for s in range(n_slabs):
        lg = None
        for t in range(t_len):
            lg_t = _dot_nt(q_bf, k_refs[t * n_slabs + s][...].astype(BF16)) * scale
            lg = lg_t if lg is None else jnp.where(row_p == t, lg_t, lg)
        past.append(lg)
    top = own[0]
    for lg in own[1:]:
        top = jnp.maximum(top, lg)
    for lg in past:
        top = jnp.maximum(top, jnp.max(lg, axis=-1, keepdims=True))
    den = jnp.zeros((rows, 1), F32)
    out = jnp.zeros(q.shape, F32)
    for j in range(t_len):
        p = jnp.exp(own[j] - top)
        den = den + p
        out = out + p * v_new[j:j + 1, :]
    row_o = lax.broadcasted_iota(jnp.int32, q.shape, 0)
    for s in range(n_slabs):
        p = jnp.exp(past[s] - top)
        den = den + jnp.sum(p, axis=-1, keepdims=True)
        p_bf = p.astype(BF16)
        for t in range(t_len):
            o_t = _dot(p_bf, v_refs[t * n_slabs + s][...].astype(BF16))
            out = out + jnp.where(row_o == t, o_t, 0.0)
    o_ref[0, 0] = out / den


def sample_attend(q, k_new, v_new, cache_k, cache_v, phys, *, t_len, n_slabs):
    dbsz, heads, rows, hd = q.shape
    page = cache_k.shape[1]
    kern = functools.partial(_sample_attend_kernel, n_slabs=n_slabs, t_len=t_len, heads=heads, scale=hd ** -0.5)
    n_gather = t_len * n_slabs
    new_spec = pl.BlockSpec((1, 1, rows, hd), lambda i, ph: (i // heads, i % heads, 0, 0))
    any_spec = pl.BlockSpec(memory_space=pl.ANY)
    return pl.pallas_call(
        kern,
        out_shape=jax.ShapeDtypeStruct((dbsz, heads, rows, hd), F32),
        grid_spec=pltpu.PrefetchScalarGridSpec(
            num_scalar_prefetch=1,
            grid=(dbsz * heads,),
            in_specs=[new_spec, new_spec, new_spec, any_spec, any_spec],
            out_specs=new_spec,
            scratch_shapes=[
                pltpu.VMEM((2, n_gather, page, hd), F32),
                pltpu.VMEM((2, n_gather, page, hd), F32),
                pltpu.SemaphoreType.DMA((2, 2 * n_gather)),
            ],
        ),
        compiler_params=_params("arbitrary"),
        name="sample_attend",
    )(phys, q, k_new, v_new, cache_k, cache_v)


def _row_tile(n, target):
    tm = min(n, target)
    assert n % tm == 0
    return tm


def kernel(x_prompt, x_sample, state_gdn, state_conv, cache_k, cache_v, page_table, norm_mix, norm_ffn,
           gdn_w_in, gdn_conv_w, gdn_a_log, gdn_dt_bias, gdn_norm, gdn_w_out, kv_norm, w_kv, moba_w_q,
           moba_w_o, moe_w_group, moe_b_group, moe_w_router, moe_b_router, moe_w_gate, moe_w_up,
           moe_w_down, norm_final):
    bsz, s, d = x_prompt.shape
    dbsz, t, _ = x_sample.shape
    depth = norm_mix.shape[0]
    n_a = gdn_w_in.shape[0]
    conv_w, qkv_dim = gdn_conv_w.shape[1:]
    gdn_heads, dv = gdn_a_log.shape[1], gdn_norm.shape[1]
    z_dim = gdn_heads * dv
    n_pool, page, moba_heads, hd = cache_k.shape
    attn_dim = moba_heads * hd
    n_pages = page_table.shape[1]
    ppb = MOBA_BLOCK // page
    n_full = n_pages // ppb
    n_groups = moe_w_group.shape[-1]
    n_experts = moe_w_router.shape[-1]
    assert n_pages == n_full * ppb, "a partial past block (tail pages) is not supported"
    assert n_full >= MOBA_TOPK and t >= conv_w - 1 and s >= conv_w - 1 and t <= page

    groups = [(x_prompt.reshape(bsz * s, d), bsz, s), (x_sample.reshape(dbsz * t, d), dbsz, t)]
    tiles = [_row_tile(bsz * s, ROW_TILE), _row_tile(dbsz * t, ROW_TILE)]

    def moe(l, x, tm):
        w_route = jnp.zeros((d, LANES), F32).at[:, :n_groups].set(moe_w_group[l])
        w_route = w_route.at[:, n_groups:n_groups + n_experts].set(moe_w_router[l])
        b_route = jnp.zeros((1, LANES), F32).at[0, :n_groups].set(moe_b_group[l])
        b_route = b_route.at[0, n_groups:n_groups + n_experts].set(moe_b_router[l])
        return hier_moe_residual(x, norm_ffn[l], w_route, b_route, moe_w_gate[l].astype(BF16),
                                 moe_w_up[l].astype(BF16), moe_w_down[l].astype(BF16),
                                 n_groups=n_groups, tm=tm)

    xs = [g[0] for g in groups]
    gdn_states, conv_states, kvs = [[], []], [[], []], [None, None]
    for l in range(depth):
        if l < n_a:
            in_dim = gdn_w_in.shape[2]
            in_pad = -(-in_dim // GDN_PROJ_COL_TILE) * GDN_PROJ_COL_TILE
            assert qkv_dim + z_dim + LANES <= in_pad and in_dim - qkv_dim - z_dim == 2 * gdn_heads
            w_in = jnp.pad(gdn_w_in[l], ((0, 0), (0, in_pad - in_dim))).astype(BF16)
            w_out = gdn_w_out[l].astype(BF16)
            for gi, (_, nb, tl) in enumerate(groups):
                tm = tiles[gi]
                proj = norm_matmul(xs[gi], norm_mix[l], w_in, tm=tm, tn=GDN_PROJ_COL_TILE).reshape(nb, tl, in_pad)
                tp = -(-tl // GDN_CHUNK) * GDN_CHUNK
                proj_p = jnp.pad(proj, ((0, 0), (0, tp - tl), (0, 0))) if tp != tl else proj
                if gi == 0:
                    conv0 = jnp.zeros((nb, conv_w - 1, qkv_dim), F32)
                    s0 = jnp.zeros((nb, gdn_heads, (qkv_dim // gdn_heads - dv) // 2, dv), F32)
                else:
                    conv0, s0 = state_conv[l], state_gdn[l]
                og, s_fin = gdn_core(proj_p, conv0, s0, gdn_conv_w[l], gdn_a_log[l], gdn_dt_bias[l],
                                     gdn_norm[l], t_valid=tl)
                og = og[:, :tl].reshape(nb * tl, z_dim)
                gdn_states[gi].append(s_fin)
                conv_states[gi].append(proj[:, tl - (conv_w - 1):, :qkv_dim])
                xs[gi] = matmul_residual(og, w_out, xs[gi], tm=tm, tn=COL_TILE)
        else:
            j = l - n_a
            w_q = moba_w_q[j].astype(BF16)
            w_o = moba_w_o[j].astype(BF16)
            q = norm_matmul(xs[0], norm_mix[l], w_q, tm=tiles[0], tn=COL_TILE).reshape(bsz, s, attn_dim)
            k_p, v_p = kvs[0]
            o = moba_prompt(q, k_p.reshape(bsz, s, attn_dim), v_p.reshape(bsz, s, attn_dim), heads=moba_heads)
            xs[0] = matmul_residual(o.reshape(bsz * s, attn_dim), w_o, xs[0], tm=tiles[0], tn=COL_TILE)
            q = norm_matmul(xs[1], norm_mix[l], w_q, tm=tiles[1], tn=COL_TILE)
            k_s, v_s = kvs[1]
            n_sel = min(MOBA_TOPK, n_full)
            phys = sample_select(q.reshape(dbsz, t * moba_heads, hd), bmean, page_table,
                                 heads=moba_heads, n_sel=n_sel, ppb=ppb)
            phys = phys[:, :, :n_sel * ppb].reshape(-1)

            def head_major(a):
                a = a.reshape(dbsz, t, moba_heads, hd).transpose(0, 2, 1, 3)
                return jnp.pad(a, ((0, 0), (0, 0), (0, -t % 8), (0, 0)))

            o = sample_attend(head_major(q), head_major(k_s), head_major(v_s), cache_k, cache_v, phys,
                              t_len=t, n_slabs=n_sel * ppb)
            o = o[:, :, :t].transpose(0, 2, 1, 3).reshape(dbsz * t, attn_dim)
            xs[1] = matmul_residual(o, w_o, xs[1], tm=tiles[1], tn=COL_TILE)
        for gi in range(2):
            xs[gi] = moe(l, xs[gi], _row_tile(xs[gi].shape[0], MOE_TOKEN_TILE))
        if l == n_a - 1:
            w_kv_bf = w_kv.astype(BF16)
            for gi, (_, nb, tl) in enumerate(groups):
                kv = norm_matmul(xs[gi], kv_norm, w_kv_bf, tm=tiles[gi], tn=COL_TILE)
                kvs[gi] = (kv[:, :attn_dim].reshape(nb, tl, moba_heads, hd),
                           kv[:, attn_dim:].reshape(nb, tl, moba_heads, hd))
            bmean = paged_block_means(cache_k, page_table, n_full=n_full, ppb=ppb)
    y_p = rmsnorm(xs[0], norm_final, tm=tiles[0]).reshape(bsz, s, d)
    y_s = rmsnorm(xs[1], norm_final, tm=tiles[1]).reshape(dbsz, t, d)
    return (y_p, y_s, jnp.stack(gdn_states[0]), jnp.stack(conv_states[0]), kvs[0][0], kvs[0][1],
            jnp.stack(gdn_states[1]), jnp.stack(conv_states[1]), kvs[1][0], kvs[1][1])
```

```python
import functools

import jax
import jax.numpy as jnp
from jax import lax
from jax.experimental import pallas as pl
from jax.experimental.pallas import tpu as pltpu

RMS_EPS = 1e-6
L2_EPS = 1e-6
MOBA_BLOCK = 256
MOBA_TOPK = 3
MOBA_HEADS_PER_STEP = 4
TOPK_IN_GROUP = 2
GDN_CHUNK = 64
LANES = 128
SUBLANES = 8
ROW_TILE = 1024
COL_TILE = 1024
GDN_PROJ_COL_TILE = 11 * LANES
MOE_TOKEN_TILE = 2048
MOE_ROW_BLOCK = 256
CONV_PAD_ROWS = SUBLANES
VMEM_LIMIT = 56 * 1024 * 1024

F32 = jnp.float32
BF16 = jnp.bfloat16
HIGHEST = lax.Precision.HIGHEST
NEG_INF = float("-inf")


def _params(*sem):
    return pltpu.CompilerParams(dimension_semantics=sem, vmem_limit_bytes=VMEM_LIMIT)


def _rms(x, g):
    return x * lax.rsqrt(jnp.mean(x * x, axis=-1, keepdims=True) + RMS_EPS) * g


def _silu(x):
    return x * jax.nn.sigmoid(x)


def _softplus(x):
    return jnp.maximum(x, 0.0) + jnp.log1p(jnp.exp(-jnp.abs(x)))


def _dot(a, b):
    return jnp.dot(a, b, preferred_element_type=F32)


def _dot_nt(a, b, precision=None):
    return lax.dot_general(a, b, (((1,), (1,)), ((), ())), precision=precision, preferred_element_type=F32)


def _dot_tn(a, b):
    return lax.dot_general(a, b, (((0,), (0,)), ((), ())), preferred_element_type=F32)


def _norm_matmul_kernel(x_ref, g_ref, w_ref, o_ref, xn_ref):
    @pl.when(pl.program_id(1) == 0)
    def _():
        xn_ref[...] = _rms(x_ref[...], g_ref[...]).astype(BF16)

    o_ref[...] = _dot(xn_ref[...], w_ref[...])


def norm_matmul(x, g, w, *, tm, tn):
    n, d = x.shape
    m = w.shape[1]
    assert n % tm == 0 and m % tn == 0
    return pl.pallas_call(
        _norm_matmul_kernel,
        out_shape=jax.ShapeDtypeStruct((n, m), F32),
        grid=(n // tm, m // tn),
        in_specs=[
            pl.BlockSpec((tm, d), lambda i, j: (i, 0)),
            pl.BlockSpec((1, d), lambda i, j: (0, 0)),
            pl.BlockSpec((d, tn), lambda i, j: (0, j)),
        ],
        out_specs=pl.BlockSpec((tm, tn), lambda i, j: (i, j)),
        scratch_shapes=[pltpu.VMEM((tm, d), BF16)],
        compiler_params=_params("parallel", "arbitrary"),
        name="norm_matmul",
    )(x, g.reshape(1, d), w)


def _matmul_residual_kernel(a_ref, w_ref, r_ref, o_ref):
    o_ref[...] = r_ref[...] + _dot(a_ref[...].astype(BF16), w_ref[...])


def matmul_residual(a, w, resid, *, tm, tn):
    n, k = a.shape
    m = w.shape[1]
    assert n % tm == 0 and m % tn == 0
    return pl.pallas_call(
        _matmul_residual_kernel,
        out_shape=jax.ShapeDtypeStruct((n, m), F32),
        grid=(n // tm, m // tn),
        in_specs=[
            pl.BlockSpec((tm, k), lambda i, j: (i, 0)),
            pl.BlockSpec((k, tn), lambda i, j: (0, j)),
            pl.BlockSpec((tm, tn), lambda i, j: (i, j)),
        ],
        out_specs=pl.BlockSpec((tm, tn), lambda i, j: (i, j)),
        compiler_params=_params("parallel", "parallel"),
        name="matmul_residual",
    )(a, w, resid)


def _gdn_kernel(page_ids_ref, qkv_ref, z_ref, ba_ref, cprev_ref, s0_ref, cw_ref, par_ref, ng_ref, *refs,
                chunk, heads, dk, dv, t_valid, n_pages_step, ppb):
    del page_ids_ref
    page_refs = refs[:n_pages_step]
    if n_pages_step:
        og_ref, s_ref, bm_ref, xbuf_ref = refs[n_pages_step:]
        for n in range(n_pages_step // ppb):
            acc = None
            for p in range(ppb):
                part = jnp.sum(page_refs[n * ppb + p][0], axis=0)
                acc = part if acc is None else acc + part
            bm_ref[0, n] = acc / MOBA_BLOCK
    else:
        og_ref, s_ref, xbuf_ref = refs
    c = pl.program_id(1)
    hk = heads * dk
    hist = CONV_PAD_ROWS
    conv_w = cw_ref.shape[0]

    @pl.when(c == 0)
    def _():
        xbuf_ref[0:hist, :] = cprev_ref[0]
        s_ref[...] = s0_ref[...]

    xbuf_ref[hist:hist + chunk, :] = qkv_ref[0]

    row = lax.broadcasted_iota(jnp.int32, (chunk, LANES), 0)
    valid = (row + c * chunk) < t_valid
    ba = ba_ref[0]
    neg_a = -jnp.exp(par_ref[0:1, :])
    dt_bias = par_ref[1:2, :]
    beta_all = jnp.where(valid, jax.nn.sigmoid(ba), 0.0)
    g_all = jnp.where(valid, neg_a * _softplus(ba + dt_bias), 0.0)

    r_i = lax.broadcasted_iota(jnp.int32, (chunk, chunk), 0)
    c_i = lax.broadcasted_iota(jnp.int32, (chunk, chunk), 1)
    incl = c_i <= r_i
    strict = c_i < r_i
    gcum_all = jnp.dot(incl.astype(F32), g_all, precision=HIGHEST, preferred_element_type=F32)

    def conv_silu(col, width):
        acc = None
        for i in range(conv_w):
            lo = hist - (conv_w - 1) + i
            term = xbuf_ref[lo:lo + chunk, col:col + width] * cw_ref[i:i + 1, col:col + width]
            acc = term if acc is None else acc + term
        return _silu(acc)

    def l2n(x):
        return x * lax.rsqrt(jnp.sum(x * x, axis=-1, keepdims=True) + L2_EPS)

    gpad =jnp.concatenate([gcum_all, jnp.zeros((LANES - chunk, LANES), F32)], axis=0) if chunk < LANES else gcum_all
    gcum_t = gpad.T

    hs = range(heads)
    q = [l2n(conv_silu(h * dk, dk)) * (dk ** -0.5) for h in hs]
    k = [l2n(conv_silu(hk + h * dk, dk)) for h in hs]
    v = [conv_silu(2 * hk + h * dv, dv) for h in hs]
    xbuf_ref[0:hist, :] = xbuf_ref[chunk:chunk + hist, :]
    q_bf = [x.astype(BF16) for x in q]
    k_bf = [x.astype(BF16) for x in k]
    beta = [beta_all[:, h:h + 1] for h in hs]
    gc = [gcum_all[:, heads + h:heads + h + 1] for h in hs]
    decay = [jnp.exp(jnp.where(incl, gc[h] - gcum_t[heads + h:heads + h + 1, :chunk], NEG_INF)) for h in hs]
    egc = [jnp.exp(x) for x in gc]
    s = [s_ref[0, h] for h in hs]
    s_bf = [x.astype(BF16) for x in s]
    kk = [_dot_nt(k_bf[h], k_bf[h]) for h in hs]
    k_s = [_dot(k_bf[h], s_bf[h]) for h in hs]
    q_s = [_dot(q_bf[h], s_bf[h]) for h in hs]
    qk = [(_dot_nt(q_bf[h], k_bf[h]) * decay[h]).astype(BF16) for h in hs]
    power = [jnp.where(strict, beta[h] * decay[h] * kk[h], 0.0).astype(BF16) for h in hs]
    u = [beta[h] * (v[h] - egc[h] * k_s[h]) for h in hs]
    powers = [power]
    span = 2
    while span < chunk:
        powers.append([_dot(p, p).astype(BF16) for p in powers[-1]])
        span *= 2
    for power in reversed(powers[1:]):
        u = [u[h] + _dot(power[h], u[h].astype(BF16)) for h in hs]
    u = [u[h] - _dot(powers[0][h], u[h].astype(BF16)) for h in hs]
    u_bf = [x.astype(BF16) for x in u]
    o = [egc[h] * q_s[h] + _dot(qk[h], u_bf[h]) for h in hs]
    glast = [x[chunk - 1:chunk, :] for x in gc]
    kd = [(k[h] * jnp.exp(glast[h] - gc[h])).astype(BF16) for h in hs]
    for h in hs:
        s_ref[0, h] = jnp.exp(glast[h]) * s[h] + _dot_tn(kd[h], u_bf[h])
    for h in hs:
        zg = _silu(z_ref[0, :, h * dv:(h + 1) * dv])
        on = o[h] * lax.rsqrt(jnp.mean(o[h] * o[h], axis=-1, keepdims=True) + RMS_EPS)
        og_ref[0, :, h * dv:(h + 1) * dv] = on * ng_ref[...] * zg


def gdn_core(proj, conv_prev, s0, conv_w, a_log, dt_bias, norm_g, *, t_valid, cache_k=None, page_ids=None):
    bsz, tp, _ = proj.shape
    heads = a_log.shape[0]
    dv = norm_g.shape[0]
    qkv_dim = conv_w.shape[1]
    dk = (qkv_dim // heads - dv) // 2
    z_dim = heads * dv
    chunk = GDN_CHUNK
    assert tp % chunk == 0 and dk == LANES and dv == LANES and 2 * heads <= LANES
    assert conv_w.shape[0] - 1 <= CONV_PAD_ROWS <= chunk
    par = jnp.zeros((8, LANES), F32)
    par = par.at[0, heads:2 * heads].set(a_log).at[1, heads:2 * heads].set(dt_bias)
    cprev = jnp.pad(conv_prev, ((0, 0), (CONV_PAD_ROWS - conv_prev.shape[1], 0), (0, 0)))
    nq = qkv_dim // LANES
    n_chunks = tp // chunk
    out_shape = [jax.ShapeDtypeStruct((bsz, tp, z_dim), F32), jax.ShapeDtypeStruct((bsz, heads, dk, dv), F32)]
    out_specs = [pl.BlockSpec((1, chunk, z_dim), lambda b, c, ids: (b, c, 0)),
                 pl.BlockSpec((1, heads, dk, dv), lambda b, c, ids: (b, 0, 0, 0))]
    page_specs, page_args, n_pages_step, ppb = [], [], 0, 1
    if cache_k is None:
        page_ids = jnp.zeros((1,), jnp.int32)
    else:
        _, page, kv_heads, hd = cache_k.shape
        ppb = MOBA_BLOCK // page
        n_pages_step = page_ids.shape[0] // (bsz * n_chunks)
        assert n_pages_step * bsz * n_chunks == page_ids.shape[0] and n_pages_step % ppb == 0

        def page_spec(i):
            return pl.BlockSpec((1, page, kv_heads, hd),
                                lambda b, c, ids: (ids[(b * n_chunks + c) * n_pages_step + i], 0, 0, 0))

        page_specs = [page_spec(i) for i in range(n_pages_step)]
        page_args = [cache_k] * n_pages_step
        out_shape.append(jax.ShapeDtypeStruct((bsz * n_chunks, n_pages_step // ppb, kv_heads, hd), F32))
        out_specs.append(pl.BlockSpec((1, n_pages_step // ppb, kv_heads, hd),
                                      lambda b, c, ids: (b * n_chunks + c, 0, 0, 0)))
    kern = functools.partial(_gdn_kernel, chunk=chunk, heads=heads, dk=dk, dv=dv, t_valid=t_valid,
                             n_pages_step=n_pages_step, ppb=ppb)
    outs = pl.pallas_call(
        kern,
        out_shape=tuple(out_shape),
        grid_spec=pltpu.PrefetchScalarGridSpec(
            num_scalar_prefetch=1,
            grid=(bsz, n_chunks),
            in_specs=[
                pl.BlockSpec((1, chunk, qkv_dim), lambda b, c, ids: (b, c, 0)),
                pl.BlockSpec((1, chunk, z_dim), lambda b, c, ids: (b, c, qkv_dim // z_dim)),
                pl.BlockSpec((1, chunk, LANES), lambda b, c, ids: (b, c, nq + z_dim // LANES)),
                pl.BlockSpec((1, CONV_PAD_ROWS, qkv_dim), lambda b, c, ids: (b, 0, 0)),
                pl.BlockSpec((1, heads, dk, dv), lambda b, c, ids: (b, 0, 0, 0)),
                pl.BlockSpec(conv_w.shape, lambda b, c, ids: (0, 0)),
                pl.BlockSpec((8, LANES), lambda b, c, ids: (0, 0)),
                pl.BlockSpec((1, dv), lambda b, c, ids: (0, 0)),
            ] + page_specs,
            out_specs=tuple(out_specs),
            scratch_shapes=[pltpu.VMEM((CONV_PAD_ROWS + chunk, qkv_dim), F32)],
        ),
        compiler_params=_params("parallel", "arbitrary"),
        name="gdn_core",
    )(page_ids, proj, proj, proj, cprev, s0, conv_w, par, norm_g.reshape(1, dv), *page_args)
    if cache_k is None:
        return outs[0], outs[1], None
    return outs[0], outs[1], outs[2].reshape(-1, kv_heads, hd)


def _moe_route_kernel(x_ref, g_ref, wr_ref, br_ref, pos_ref, wts_ref, seg_ref, *, n_groups, n_experts):
    tm = x_ref.shape[0]
    epg = n_experts // n_groups
    xn = _rms(x_ref[...], g_ref[...])
    logits = jnp.dot(xn, wr_ref[...], precision=HIGHEST, preferred_element_type=F32) + br_ref[...]
    lt = logits.T
    sub = lax.broadcasted_iota(jnp.int32, lt.shape, 0)
    is_g = sub < n_groups
    gl = jnp.where(is_g, lt, NEG_INF)
    ge = jnp.exp(gl - jnp.max(gl, axis=0, keepdims=True))
    pg = ge / jnp.sum(ge, axis=0, keepdims=True)
    pg_sel = jnp.max(pg, axis=0, keepdims=True)
    g_idx = jnp.min(jnp.where(is_g & (pg == pg_sel), sub, LANES), axis=0, keepdims=True)
    lo = n_groups + g_idx * epg
    in_g = (sub >= lo) & (sub < lo + epg)
    el = jnp.where(in_g, lt, NEG_INF)
    ee = jnp.exp(el - jnp.max(el, axis=0, keepdims=True))
    pe = ee / jnp.sum(ee, axis=0, keepdims=True)
    rest = jnp.where(in_g, pe, NEG_INF)
    tops = []
    for _ in range(TOPK_IN_GROUP):
        top_v = jnp.max(rest, axis=0, keepdims=True)
        top_i = jnp.min(jnp.where(rest == top_v, sub, LANES), axis=0, keepdims=True)
        tops.append((top_v, top_i))
        rest = jnp.where(sub == top_i, NEG_INF, rest)
    denom = tops[0][0]
    for top_v, _ in tops[1:]:
        denom = denom + top_v
    wts_ref[0] = jnp.concatenate([pg_sel * top_v / denom for top_v, _ in tops], axis=1)

    onehot = [jnp.where(sub == top_i, 1.0, 0.0) for _, top_i in tops]
    span = min(tm, 512)
    earlier = jnp.where(lax.broadcasted_iota(jnp.int32, (span, span), 0)
                        < lax.broadcasted_iota(jnp.int32, (span, span), 1), 1.0, 0.0).astype(BF16)
    count = [jnp.sum(oh, axis=1, keepdims=True) for oh in onehot]
    total = count[0]
    for c in count[1:]:
        total = total + c
    padded = jnp.floor((total + (SUBLANES - 1)) * (1.0 / SUBLANES)) * SUBLANES
    below = jnp.where(lax.broadcasted_iota(jnp.int32, (LANES, LANES), 1) < lax.broadcasted_iota(jnp.int32, (LANES, LANES), 0),
                      1.0, 0.0)
    start = jnp.dot(below, jnp.broadcast_to(padded, (LANES, LANES)), precision=HIGHEST,
                    preferred_element_type=F32)[:, 0:1]
    pos = []
    base = start
    for oh, c in zip(onehot, count):
        for lo_n in range(0, tm, span):
            oh_n = oh[:, lo_n:lo_n + span]
            before = _dot(oh_n.astype(BF16), earlier)
            pos.append(jnp.sum(oh_n * (base + before), axis=0, keepdims=True))
            base = base + jnp.sum(oh_n, axis=1, keepdims=True)
    pos_ref[0] = jnp.concatenate(pos, axis=1).astype(jnp.int32)
    lane = lax.broadcasted_iota(jnp.int32, (LANES, LANES), 1)
    cols = jnp.where(lane == 0, start, 0.0) + jnp.where(lane == 1, total, 0.0)
    seg_ref[0] = cols.T[0:8, :].astype(jnp.int32)


def _moe_expert_kernel(pos_ref, wts_ref, seg_ref, x_ref, g_ref, g_out_ref, wg_ref, wu_ref, wd_ref, o_ref, rows_ref,
                       *, n_groups, n_experts, n_slots, row_block, norm_out):
    e = pl.program_id(1)
    tm, d = x_ref.shape

    @pl.when(e == 0)
    def _():
        o_ref[...] = _rms(x_ref[...], g_ref[...])
        end = 0
        for j in range(n_experts):
            end = seg_ref[0, 0, n_groups + j] + seg_ref[0, 1, n_groups + j]
            last_tile = pl.multiple_of(jnp.maximum((end - 1) // SUBLANES * SUBLANES, 0), SUBLANES)
            rows_ref[pl.ds(last_tile, SUBLANES), :] = jnp.zeros((SUBLANES, d), F32)
        tail = pl.multiple_of((end + SUBLANES - 1) // SUBLANES * SUBLANES, SUBLANES)
        rows_ref[pl.ds(tail, row_block), :] = jnp.zeros((row_block, d), F32)

        def scatter(n, carry):
            row = o_ref[pl.ds(n, 1), :]
            for k in range(n_slots):
                rows_ref[pl.ds(pos_ref[0, 0, k * tm + n], 1), :] = row
            return carry

        lax.fori_loop(0, tm, scatter, 0, unroll=8)

    start = seg_ref[0, 0, n_groups + e]
    count = seg_ref[0, 1, n_groups + e]

    def run_block(i, carry):
        r0 = pl.multiple_of(start + i * row_block, SUBLANES)
        xin = rows_ref[pl.ds(r0, row_block), :]
        xb = xin.astype(BF16)
        he = _silu(_dot(xb, wg_ref[0])) * _dot(xb, wu_ref[0])
        y = _dot(he.astype(BF16), wd_ref[0])
        mine = lax.broadcasted_iota(jnp.int32, (row_block, 1), 0) < count - i * row_block
        rows_ref[pl.ds(r0, row_block), :] = jnp.where(mine, y, xin)
        return carry

    lax.fori_loop(0, (count + row_block - 1) // row_block, run_block, 0)

    @pl.when(e == n_experts - 1)
    def _():
        def gather(n, carry):
            acc = x_ref[pl.ds(n, 1), :]
            for k in range(n_slots):
                acc = acc + wts_ref[0, 0, k * tm + n] * rows_ref[pl.ds(pos_ref[0, 0, k * tm + n], 1), :]
            o_ref[pl.ds(n, 1), :] = acc
            return carry

        lax.fori_loop(0, tm, gather, 0, unroll=8)
        if norm_out:
            o_ref[...] = _rms(o_ref[...], g_out_ref[...])


def hier_moe_residual(x, g, w_route, b_route, w_gate, w_up, w_down, *, n_groups, tm, g_out=None):
    n, d = x.shape
    n_experts, _, ff = w_gate.shape
    n_slots = TOPK_IN_GROUP
    assert n % tm == 0 and tm % LANES == 0 and n_groups + n_experts <= LANES
    n_tiles = n // tm
    row_block = min(MOE_ROW_BLOCK, tm)
    rows_cap = -(-(n_slots * tm + n_experts * (SUBLANES - 1)) // SUBLANES) * SUBLANES + row_block
    route =functools.partial(_moe_route_kernel, n_groups=n_groups, n_experts=n_experts)
    pos, wts, seg = pl.pallas_call(
        route,
        out_shape=(jax.ShapeDtypeStruct((n_tiles, 1, n_slots * tm), jnp.int32),
                   jax.ShapeDtypeStruct((n_tiles, 1, n_slots * tm), F32),
                   jax.ShapeDtypeStruct((n_tiles, 8, LANES), jnp.int32)),
        grid=(n_tiles,),
        in_specs=[
            pl.BlockSpec((tm, d), lambda i: (i, 0)),
            pl.BlockSpec((1, d), lambda i: (0, 0)),
            pl.BlockSpec((d, LANES), lambda i: (0, 0)),
            pl.BlockSpec((1, LANES), lambda i: (0, 0)),
        ],
        out_specs=(pl.BlockSpec((1, 1, n_slots * tm), lambda i: (i, 0, 0)),
                   pl.BlockSpec((1, 1, n_slots * tm), lambda i: (i, 0, 0)),
                   pl.BlockSpec((1, 8, LANES), lambda i: (i, 0, 0))),
        compiler_params=_params("parallel"),
        name="moe_route",
    )(x, g.reshape(1, d), w_route, b_route)

    def smem_spec(shape):
        return pl.BlockSpec((1,) + shape, lambda i, e: (i, 0, 0), memory_space=pltpu.SMEM)

    expert = functools.partial(_moe_expert_kernel, n_groups=n_groups, n_experts=n_experts, n_slots=n_slots,
                               row_block=row_block, norm_out=g_out is not None)
    once = pl.Buffered(1)
    return pl.pallas_call(
        expert,
        out_shape=jax.ShapeDtypeStruct((n, d), F32),
        grid=(n_tiles, n_experts),
        in_specs=[
            smem_spec((1, n_slots * tm)),
            smem_spec((1, n_slots * tm)),
            smem_spec((8, LANES)),
            pl.BlockSpec((tm, d), lambda i, e: (i, 0), pipeline_mode=once),
            pl.BlockSpec((1, d), lambda i, e: (0, 0)),
            pl.BlockSpec((1, d), lambda i, e: (0, 0)),
            pl.BlockSpec((1, d, ff), lambda i, e: (e, 0, 0)),
            pl.BlockSpec((1, d, ff), lambda i, e: (e, 0, 0)),
            pl.BlockSpec((1, ff, d), lambda i, e: (e, 0, 0)),
        ],
        out_specs=pl.BlockSpec((tm, d), lambda i, e: (i, 0), pipeline_mode=once),
        scratch_shapes=[pltpu.VMEM((rows_cap, d), F32)],
        compiler_params=_params("parallel", "arbitrary"),
        name="moe_experts",
    )(pos, wts, seg, x, g.reshape(1, d), (g if g_out is None else g_out).reshape(1, d), w_gate, w_up, w_down)


def _moba_prompt_kernel(q_ref, k_ref, v_ref, o_ref, kbf_ref, vt_ref, kmean_ref, sel_ref, lg_ref,
                        *, n_blocks, n_sel, scale):
    qi = pl.program_id(2)
    blk = MOBA_BLOCK
    hd = kbf_ref.shape[2]
    hs = range(q_ref.shape[2] // hd)
    nb_pad = kmean_ref.shape[0] // len(hs)

    @pl.when(qi == 0)
    def _():
        kmean_ref[...] = jnp.zeros_like(kmean_ref)
        for h in hs:
            for j in range(n_blocks):
                kj = k_ref[0, j * blk:(j + 1) * blk, h * hd:(h + 1) * hd]
                kmean_ref[h * nb_pad + j:h * nb_pad + j + 1, :] = jnp.mean(kj, axis=0, keepdims=True)
                kbf_ref[h * n_blocks + j] = kj.astype(BF16)
                vt_ref[h * n_blocks + j] = v_ref[0, j * blk:(j + 1) * blk, h * hd:(h + 1) * hd].T.astype(BF16)

    q = [q_ref[0, :, h * hd:(h + 1) * hd] for h in hs]
    q_bf = [x.astype(BF16) for x in q]
    gate = [_dot_nt(kmean_ref[h * nb_pad:(h + 1) * nb_pad, :], q[h], precision=HIGHEST) for h in hs]
    blk_i = lax.broadcasted_iota(jnp.int32, gate[0].shape, 0)
    past = blk_i < qi
    for h in hs:
        for j in range(n_blocks):
            gj = gate[h][j:j + 1, :]
            ahead = ((gate[h] > gj) | ((gate[h] == gj) & (blk_i < j))) & past
            rank = jnp.sum(jnp.where(ahead, 1.0, 0.0), axis=0, keepdims=True)
            sel_ref[h * nb_pad + j:h * nb_pad + j + 1, :] = jnp.where(rank < n_sel, 1.0, 0.0)

    def past_block(j, tops):
        new = []
        for h in hs:
            sc = _dot_nt(kbf_ref[h * n_blocks + j], q_bf[h]) * scale
            sc = jnp.where(sel_ref[pl.ds(h * nb_pad + j, 1), :] > 0.5, sc, NEG_INF)
            lg_ref[h * n_blocks + j] = sc
            new.append(jnp.maximum(tops[h], jnp.max(sc, axis=0, keepdims=True)))
        return tuple(new)

    tops = lax.fori_loop(0, qi, past_block, tuple(jnp.full((1, blk), NEG_INF, F32) for _ in hs))
    key_i = lax.broadcasted_iota(jnp.int32, (blk, blk), 0)
    qry_i = lax.broadcasted_iota(jnp.int32, (blk, blk), 1)
    top = []
    for h in hs:
        own = jnp.where(key_i <= qry_i, _dot_nt(kbf_ref[h * n_blocks + qi], q_bf[h]) * scale, NEG_INF)
        lg_ref[h * n_blocks + qi] = own
        top.append(jnp.maximum(tops[h], jnp.max(own, axis=0, keepdims=True)))

    def weigh_block(j, carry):
        new = []
        for h in hs:
            den, out = carry[h]
            p = jnp.exp(lg_ref[h * n_blocks + j] - top[h])
            new.append((den + jnp.sum(p, axis=0, keepdims=True), out + _dot(vt_ref[h * n_blocks + j], p.astype(BF16))))
        return tuple(new)

    acc = lax.fori_loop(0, qi + 1, weigh_block,
                        tuple((jnp.zeros((1, blk), F32), jnp.zeros((hd, blk), F32)) for _ in hs))
    for h in hs:
        den, out = acc[h]
        o_ref[0, :, h * hd:(h + 1) * hd] = (out / den).T


def moba_prompt(q, k, v, *, heads):
    bsz, s, hd_all = q.shape
    hd = hd_all // heads
    assert s % MOBA_BLOCK == 0 and hd == LANES
    n_blocks = s // MOBA_BLOCK
    nb_pad = -(-n_blocks // SUBLANES) * SUBLANES
    n_sel = min(MOBA_TOPK, (s - 1) // MOBA_BLOCK)
    hps = min(heads, MOBA_HEADS_PER_STEP)
    assert heads % hps == 0
    kern = functools.partial(_moba_prompt_kernel, n_blocks=n_blocks, n_sel=n_sel, scale=hd ** -0.5)
    return pl.pallas_call(
        kern,
        out_shape=jax.ShapeDtypeStruct((bsz, s, hd_all), F32),
        grid=(bsz, heads // hps, n_blocks),
        in_specs=[
            pl.BlockSpec((1, MOBA_BLOCK, hps * hd), lambda b, h, i: (b, i, h)),
            pl.BlockSpec((1, s, hps * hd), lambda b, h, i: (b, 0, h)),
            pl.BlockSpec((1, s, hps * hd), lambda b, h, i: (b, 0, h)),
        ],
        out_specs=pl.BlockSpec((1, MOBA_BLOCK, hps * hd), lambda b, h, i: (b, i, h)),
        scratch_shapes=[
            pltpu.VMEM((hps * n_blocks, MOBA_BLOCK, hd), BF16),
            pltpu.VMEM((hps * n_blocks, hd, MOBA_BLOCK), BF16),
            pltpu.VMEM((hps * nb_pad, hd), F32),
            pltpu.VMEM((hps * nb_pad, MOBA_BLOCK), F32),
            pltpu.VMEM((hps * n_blocks, MOBA_BLOCK, MOBA_BLOCK), F32),
        ],
        compiler_params=_params("parallel", "parallel", "arbitrary"),
        name="moba_prompt",
    )(q, k, v)


def _sample_select_kernel(q_ref, bm_ref, pt_ref, o_ref, *, heads, n_sel, ppb):
    rows, hd = q_ref.shape[1], q_ref.shape[2]
    nb = bm_ref.shape[2]
    n_pages = pt_ref.shape[2]
    q = q_ref[0]
    row_head = lax.broadcasted_iota(jnp.int32, (rows, nb), 0) % heads
    lane = lax.broadcasted_iota(jnp.int32, (rows, nb), 1)
    lane_pt = lax.broadcasted_iota(jnp.int32, (rows, n_pages), 1)
    lane_out = lax.broadcasted_iota(jnp.int32, (rows, LANES), 1)
    gate = jnp.zeros((rows, nb), F32)
    for h in range(heads):
        gate_h = _dot_nt(q, bm_ref[0, h], precision=HIGHEST)
        gate = jnp.where(row_head == h, gate_h, gate)
    pt = pt_ref[0].astype(F32)
    out = jnp.zeros((rows, LANES), F32)
    for s in range(n_sel):
        top = jnp.max(gate, axis=-1, keepdims=True)
        idx = jnp.min(jnp.where(gate == top, lane, nb), axis=-1, keepdims=True)
        gate = jnp.where(lane == idx, NEG_INF, gate)
        for p in range(ppb):
            phys = jnp.sum(jnp.where(lane_pt == idx * ppb + p, pt, 0.0), axis=-1, keepdims=True)
            out = out + jnp.where(lane_out == s * ppb + p, phys, 0.0)
    o_ref[0] = out.astype(jnp.int32)


def sample_select(q, bmean, page_table, *, heads, n_sel, ppb):
    dbsz, rows, hd = q.shape
    nb = bmean.shape[2]
    n_pages = page_table.shape[1]
    kern = functools.partial(_sample_select_kernel, heads=heads, n_sel=n_sel, ppb=ppb)
    return pl.pallas_call(
        kern,
        out_shape=jax.ShapeDtypeStruct((dbsz, rows, LANES), jnp.int32),
        grid=(dbsz,),
        in_specs=[
            pl.BlockSpec((1, rows, hd), lambda b: (b, 0, 0)),
            pl.BlockSpec((1, heads, nb, hd), lambda b: (b, 0, 0, 0)),
            pl.BlockSpec((1, 1, n_pages), lambda b: (b, 0, 0)),
        ],
        out_specs=pl.BlockSpec((1, rows, LANES), lambda b: (b, 0, 0)),
        compiler_params=_params("parallel"),
        name="sample_select",
    )(q, bmean, page_table.reshape(dbsz, 1, n_pages))


def _sample_attend_kernel(phys_ref, q_ref, kn_ref, vn_ref, ck_ref, cv_ref, o_ref, kbuf_ref, vbuf_ref, sem_ref,
                          *, n_slabs, t_len, heads, scale):
    n_gather = t_len * n_slabs
    step = pl.program_id(0)
    n_steps = pl.num_programs(0)
    slot = step % 2

    def slab_copies(of_step, to_slot):
        b, h = of_step // heads, of_step % heads
        copies = []
        for t in range(t_len):
            for s in range(n_slabs):
                g = t * n_slabs + s
                phys = phys_ref[((b * t_len + t) * heads + h) * n_slabs + s]
                copies.append(pltpu.make_async_copy(ck_ref.at[phys, :, h, :], kbuf_ref.at[to_slot, g],
                                                    sem_ref.at[to_slot, g]))
                copies.append(pltpu.make_async_copy(cv_ref.at[phys, :, h, :], vbuf_ref.at[to_slot, g],
                                                    sem_ref.at[to_slot, n_gather + g]))
        return copies

    @pl.when(step == 0)
    def _():
        for cp in slab_copies(step, slot):
            cp.start()

    @pl.when(step + 1 < n_steps)
    def _():
        for cp in slab_copies(step + 1, 1 - slot):
            cp.start()

    for cp in slab_copies(step, slot):
        cp.wait()
    k_refs = [kbuf_ref.at[slot, g] for g in range(n_gather)]
    v_refs = [vbuf_ref.at[slot, g] for g in range(n_gather)]

    q = q_ref[0, 0]
    rows = q.shape[0]
    q_bf = q.astype(BF16)
    k_new, v_new = kn_ref[0, 0], vn_ref[0, 0]
    row_c = lax.broadcasted_iota(jnp.int32, (rows, 1), 0)
    own = []
    for j in range(t_len):
        lg = jnp.sum(q * k_new[j:j + 1, :], axis=-1, keepdims=True) * scale
        own.append(jnp.where(row_c >= j, lg, NEG_INF))
    page = kbuf_ref.shape[2]
    row_p = lax.broadcasted_iota(jnp.int32, (rows, page), 0)
    past = []
    for s in range(n_slabs):
        lg = None
        for t in range(t_len):
            lg_t = _dot_nt(q_bf, k_refs[t * n_slabs + s][...].astype(BF16)) * scale
            lg = lg_t if lg is None else jnp.where(row_p == t, lg_t, lg)
        past.append(lg)
    top = own[0]
    for lg in own[1:]:
        top = jnp.maximum(top, lg)
    for lg in past:
        top = jnp.maximum(top, jnp.max(lg, axis=-1, keepdims=True))
    den = jnp.zeros((rows, 1), F32)
    out = jnp.zeros(q.shape, F32)
    for j in range(t_len):
        p = jnp.exp(own[j] - top)
        den = den + p
        out = out + p * v_new[j:j + 1, :]
    row_o = lax.broadcasted_iota(jnp.int32, q.shape, 0)
    for s in range(n_slabs):
        p = jnp.exp(past[s] - top)
        den = den + jnp.sum(p, axis=-1, keepdims=True)
        p_bf = p.astype(BF16)
        for t in range(t_len):
            o_t = _dot(p_bf, v_refs[t * n_slabs + s][...].astype(BF16))
            out = out + jnp.where(row_o == t, o_t, 0.0)
    o_ref[0, 0] = out / den


def sample_attend(q, k_new, v_new, cache_k, cache_v, phys, *, t_len, n_slabs):
    dbsz, heads, rows, hd = q.shape
    page = cache_k.shape[1]
    kern = functools.partial(_sample_attend_kernel, n_slabs=n_slabs, t_len=t_len, heads=heads, scale=hd ** -0.5)
    n_gather = t_len * n_slabs
    new_spec = pl.BlockSpec((1, 1, rows, hd), lambda i, ph: (i // heads, i % heads, 0, 0))
    any_spec = pl.BlockSpec(memory_space=pl.ANY)
    return pl.pallas_call(
        kern,
        out_shape=jax.ShapeDtypeStruct((dbsz, heads, rows, hd), F32),
        grid_spec=pltpu.PrefetchScalarGridSpec(
            num_scalar_prefetch=1,
            grid=(dbsz * heads,),
            in_specs=[new_spec, new_spec, new_spec, any_spec, any_spec],
            out_specs=new_spec,
            scratch_shapes=[
                pltpu.VMEM((2, n_gather, page, hd), F32),
                pltpu.VMEM((2, n_gather, page, hd), F32),
                pltpu.SemaphoreType.DMA((2, 2 * n_gather)),
            ],
        ),
        compiler_params=_params("arbitrary"),
        name="sample_attend",
    )(phys, q, k_new, v_new, cache_k, cache_v)


def _row_tile(n, target):
    tm = min(n, target)
    assert n % tm == 0
    return tm


def kernel(x_prompt, x_sample, state_gdn, state_conv, cache_k, cache_v, page_table, norm_mix, norm_ffn,
           gdn_w_in, gdn_conv_w, gdn_a_log, gdn_dt_bias, gdn_norm, gdn_w_out, kv_norm, w_kv, moba_w_q,
           moba_w_o, moe_w_group, moe_b_group, moe_w_router, moe_b_router, moe_w_gate, moe_w_up,
           moe_w_down, norm_final):
    bsz, s, d = x_prompt.shape
    dbsz, t, _ = x_sample.shape
    depth = norm_mix.shape[0]
    n_a = gdn_w_in.shape[0]
    conv_w, qkv_dim = gdn_conv_w.shape[1:]
    gdn_heads, dv = gdn_a_log.shape[1], gdn_norm.shape[1]
    z_dim = gdn_heads * dv
    n_pool, page, moba_heads, hd = cache_k.shape
    attn_dim = moba_heads * hd
    n_pages = page_table.shape[1]
    ppb = MOBA_BLOCK // page
    n_full = n_pages // ppb
    n_groups = moe_w_group.shape[-1]
    n_experts = moe_w_router.shape[-1]
    assert n_pages == n_full * ppb, "a partial past block (tail pages) is not supported"
    assert n_full >= MOBA_TOPK and t >= conv_w - 1 and s >= conv_w - 1 and t <= page

    groups = [(x_prompt.reshape(bsz * s, d), bsz, s), (x_sample.reshape(dbsz * t, d), dbsz, t)]
    tiles = [_row_tile(bsz * s, ROW_TILE), _row_tile(dbsz * t, ROW_TILE)]

    def moe(l, x, tm):
        w_route = jnp.zeros((d, LANES), F32).at[:, :n_groups].set(moe_w_group[l])
        w_route = w_route.at[:, n_groups:n_groups + n_experts].set(moe_w_router[l])
        b_route = jnp.zeros((1, LANES), F32).at[0, :n_groups].set(moe_b_group[l])
        b_route = b_route.at[0, n_groups:n_groups + n_experts].set(moe_b_router[l])
        return hier_moe_residual(x, norm_ffn[l], w_route, b_route, moe_w_gate[l].astype(BF16),
                                 moe_w_up[l].astype(BF16), moe_w_down[l].astype(BF16),
                                 n_groups=n_groups, tm=tm, g_out=norm_final if l == depth - 1 else None)

    xs = [g[0] for g in groups]
    gdn_states, conv_states, kvs = [[], []], [[], []], [None, None]
    for l in range(depth):
        if l < n_a:
            in_dim = gdn_w_in.shape[2]
            in_pad = -(-in_dim // GDN_PROJ_COL_TILE) * GDN_PROJ_COL_TILE
            assert qkv_dim + z_dim + LANES <= in_pad and in_dim - qkv_dim - z_dim == 2 * gdn_heads
            w_in = jnp.pad(gdn_w_in[l], ((0, 0), (0, in_pad - in_dim))).astype(BF16)
            w_out = gdn_w_out[l].astype(BF16)
            for gi, (_, nb, tl) in enumerate(groups):
                tm = tiles[gi]
                proj = norm_matmul(xs[gi], norm_mix[l], w_in, tm=tm, tn=GDN_PROJ_COL_TILE).reshape(nb, tl, in_pad)
                tp = -(-tl // GDN_CHUNK) * GDN_CHUNK
                proj_p = jnp.pad(proj, ((0, 0), (0, tp - tl), (0, 0))) if tp != tl else proj
                if gi == 0:
                    conv0 = jnp.zeros((nb, conv_w - 1, qkv_dim), F32)
                    s0 = jnp.zeros((nb, gdn_heads, (qkv_dim // gdn_heads - dv) // 2, dv), F32)
                else:
                    conv0, s0 = state_conv[l], state_gdn[l]
                with_means = gi == 0 and l == 0
                og, s_fin, means = gdn_core(proj_p, conv0, s0, gdn_conv_w[l], gdn_a_log[l], gdn_dt_bias[l],
                                            gdn_norm[l], t_valid=tl, cache_k=cache_k if with_means else None,
                                            page_ids=page_table.reshape(-1) if with_means else None)
                if with_means:
                    bmean = means.reshape(dbsz, n_full, moba_heads, hd).transpose(0, 2, 1, 3)
                og = og[:, :tl].reshape(nb * tl, z_dim)
                gdn_states[gi].append(s_fin)
                conv_states[gi].append(proj[:, tl - (conv_w - 1):, :qkv_dim])
                xs[gi] = matmul_residual(og, w_out, xs[gi], tm=tm, tn=COL_TILE)
        else:
            j = l - n_a
            w_q = moba_w_q[j].astype(BF16)
            w_o = moba_w_o[j].astype(BF16)
            q = norm_matmul(xs[0], norm_mix[l], w_q, tm=tiles[0], tn=COL_TILE).reshape(bsz, s, attn_dim)
            k_p, v_p = kvs[0]
            o = moba_prompt(q, k_p.reshape(bsz, s, attn_dim), v_p.reshape(bsz, s, attn_dim), heads=moba_heads)
            xs[0] = matmul_residual(o.reshape(bsz * s, attn_dim), w_o, xs[0], tm=tiles[0], tn=COL_TILE)
            q = norm_matmul(xs[1], norm_mix[l], w_q, tm=tiles[1], tn=COL_TILE)
            k_s, v_s = kvs[1]
            n_sel = min(MOBA_TOPK, n_full)
            phys = sample_select(q.reshape(dbsz, t * moba_heads, hd), bmean, page_table,
                                 heads=moba_heads, n_sel=n_sel, ppb=ppb)
            phys = phys[:, :, :n_sel * ppb].reshape(-1)

            def head_major(a):
                a = a.reshape(dbsz, t, moba_heads, hd).transpose(0, 2, 1, 3)
                return jnp.pad(a, ((0, 0), (0, 0), (0, -t % 8), (0, 0)))

            o = sample_attend(head_major(q), head_major(k_s), head_major(v_s), cache_k, cache_v, phys,
                              t_len=t, n_slabs=n_sel * ppb)
            o = o[:, :, :t].transpose(0, 2, 1, 3).reshape(dbsz * t, attn_dim)
            xs[1] = matmul_residual(o, w_o, xs[1], tm=tiles[1], tn=COL_TILE)
        for gi in range(2):
            xs[gi] = moe(l, xs[gi], _row_tile(xs[gi].shape[0], MOE_TOKEN_TILE))
        if l == n_a - 1:
            w_kv_bf = w_kv.astype(BF16)
            for gi, (_, nb, tl) in enumerate(groups):
                kv = norm_matmul(xs[gi], kv_norm, w_kv_bf, tm=tiles[gi], tn=COL_TILE)
                kvs[gi] = (kv[:, :attn_dim].reshape(nb, tl, moba_heads, hd),
                           kv[:, attn_dim:].reshape(nb, tl, moba_heads, hd))
    y_p = xs[0].reshape(bsz, s, d)
    y_s = xs[1].reshape(dbsz, t, d)
    return (y_p, y_s, jnp.stack(gdn_states[0]), jnp.stack(conv_states[0]), kvs[0][0], kvs[0][1],
            jnp.stack(gdn_states[1]), jnp.stack(conv_states[1]), kvs[1][0], kvs[1][1])
```

```python
import functools

import jax
import jax.numpy as jnp
from jax import lax
from jax.experimental import pallas as pl
from jax.experimental.pallas import tpu as pltpu

RMS_EPS = 1e-6
L2_EPS = 1e-6
MOBA_BLOCK = 256
MOBA_TOPK = 3
MOBA_HEADS_PER_STEP = 4
TOPK_IN_GROUP = 2
GDN_CHUNK = 64
LANES = 128
SUBLANES = 8
ROW_TILE = 1024
COL_TILE = 1024
GDN_PROJ_COL_TILE = 11 * LANES
MOE_TOKEN_TILE = 2048
MOE_ROW_BLOCK = 256
MOE_EXPERTS_PER_STEP = 2
CONV_PAD_ROWS = SUBLANES
VMEM_LIMIT = 56 * 1024 * 1024

F32 = jnp.float32
BF16 = jnp.bfloat16
HIGHEST = lax.Precision.HIGHEST
NEG_INF = float("-inf")


def _params(*sem):
    return pltpu.CompilerParams(dimension_semantics=sem, vmem_limit_bytes=VMEM_LIMIT)


def _rms(x, g):
    return x * lax.rsqrt(jnp.mean(x * x, axis=-1, keepdims=True) + RMS_EPS) * g


def _silu(x):
    return x * jax.nn.sigmoid(x)


def _softplus(x):
    return jnp.maximum(x, 0.0) + jnp.log1p(jnp.exp(-jnp.abs(x)))


def _dot(a, b):
    return jnp.dot(a, b, preferred_element_type=F32)


def _dot_nt(a, b, precision=None):
    return lax.dot_general(a, b, (((1,), (1,)), ((), ())), precision=precision, preferred_element_type=F32)


def _dot_tn(a, b):
    return lax.dot_general(a, b, (((0,), (0,)), ((), ())), preferred_element_type=F32)


def _norm_matmul_kernel(x_ref, g_ref, w_ref, o_ref, xn_ref):
    @pl.when(pl.program_id(1) == 0)
    def _():
        xn_ref[...] = _rms(x_ref[...], g_ref[...]).astype(BF16)

    o_ref[...] = _dot(xn_ref[...], w_ref[...])


def norm_matmul(x, g, w, *, tm, tn):
    n, d = x.shape
    m = w.shape[1]
    assert n % tm == 0 and m % tn == 0
    return pl.pallas_call(
        _norm_matmul_kernel,
        out_shape=jax.ShapeDtypeStruct((n, m), F32),
        grid=(n // tm, m // tn),
        in_specs=[
            pl.BlockSpec((tm, d), lambda i, j: (i, 0)),
            pl.BlockSpec((1, d), lambda i, j: (0, 0)),
            pl.BlockSpec((d, tn), lambda i, j: (0, j)),
        ],
        out_specs=pl.BlockSpec((tm, tn), lambda i, j: (i, j)),
        scratch_shapes=[pltpu.VMEM((tm, d), BF16)],
        compiler_params=_params("parallel", "arbitrary"),
        name="norm_matmul",
    )(x, g.reshape(1, d), w)


def _matmul_residual_kernel(a_ref, w_ref, r_ref, o_ref):
    o_ref[...] = r_ref[...] + _dot(a_ref[...].astype(BF16), w_ref[...])


def matmul_residual(a, w, resid, *, tm, tn):
    n, k = a.shape
    m = w.shape[1]
    assert n % tm == 0 and m % tn == 0
    return pl.pallas_call(
        _matmul_residual_kernel,
        out_shape=jax.ShapeDtypeStruct((n, m), F32),
        grid=(n // tm, m // tn),
        in_specs=[
            pl.BlockSpec((tm, k), lambda i, j: (i, 0)),
            pl.BlockSpec((k, tn), lambda i, j: (0, j)),
            pl.BlockSpec((tm, tn), lambda i, j: (i, j)),
        ],
        out_specs=pl.BlockSpec((tm, tn), lambda i, j: (i, j)),
        compiler_params=_params("parallel", "parallel"),
        name="matmul_residual",
    )(a, w, resid)


def _gdn_kernel(page_ids_ref, qkv_ref, z_ref, ba_ref, cprev_ref, s0_ref, cw_ref, par_ref, ng_ref, *refs,
                chunk, heads, dk, dv, t_valid, n_pages_step, ppb):
    del page_ids_ref
    page_refs = refs[:n_pages_step]
    if n_pages_step:
        og_ref, s_ref, bm_ref, xbuf_ref = refs[n_pages_step:]
        for n in range(n_pages_step // ppb):
            acc = None
            for p in range(ppb):
                part = jnp.sum(page_refs[n * ppb + p][0], axis=0)
                acc = part if acc is None else acc + part
            bm_ref[0, n] = acc / MOBA_BLOCK
    else:
        og_ref, s_ref, xbuf_ref = refs
    c = pl.program_id(1)
    hk = heads * dk
    hist = CONV_PAD_ROWS
    conv_w = cw_ref.shape[0]

    @pl.when(c == 0)
    def _():
        xbuf_ref[0:hist, :] = cprev_ref[0]
        s_ref[...] = s0_ref[...]

    xbuf_ref[hist:hist + chunk, :] = qkv_ref[0]

    row = lax.broadcasted_iota(jnp.int32, (chunk, LANES), 0)
    valid = (row + c * chunk) < t_valid
    ba = ba_ref[0]
    neg_a = -jnp.exp(par_ref[0:1, :])
    dt_bias = par_ref[1:2, :]
    beta_all = jnp.where(valid, jax.nn.sigmoid(ba), 0.0)
    g_all = jnp.where(valid, neg_a * _softplus(ba + dt_bias), 0.0)

    r_i = lax.broadcasted_iota(jnp.int32, (chunk, chunk), 0)
    c_i = lax.broadcasted_iota(jnp.int32, (chunk, chunk), 1)
    incl = c_i <= r_i
    strict = c_i < r_i
    gcum_all = jnp.dot(incl.astype(F32), g_all, precision=HIGHEST, preferred_element_type=F32)

    def conv_silu(col, width):
        acc = None
        for i in range(conv_w):
            lo = hist - (conv_w - 1) + i
            term = xbuf_ref[lo:lo + chunk, col:col + width] * cw_ref[i:i + 1, col:col + width]
            acc = term if acc is None else acc + term
        return _silu(acc)

    def l2n(x):
        return x * lax.rsqrt(jnp.sum(x * x, axis=-1, keepdims=True) + L2_EPS)

    gpad =jnp.concatenate([gcum_all, jnp.zeros((LANES - chunk, LANES), F32)], axis=0) if chunk < LANES else gcum_all
    gcum_t = gpad.T

    hs = range(heads)
    q = [l2n(conv_silu(h * dk, dk)) * (dk ** -0.5) for h in hs]
    k = [l2n(conv_silu(hk + h * dk, dk)) for h in hs]
    v = [conv_silu(2 * hk + h * dv, dv) for h in hs]
    xbuf_ref[0:hist, :] = xbuf_ref[chunk:chunk + hist, :]
    q_bf = [x.astype(BF16) for x in q]
    k_bf = [x.astype(BF16) for x in k]
    beta = [beta_all[:, h:h + 1] for h in hs]
    gc = [gcum_all[:, heads + h:heads + h + 1] for h in hs]
    decay = [jnp.exp(jnp.where(incl, gc[h] - gcum_t[heads + h:heads + h + 1, :chunk], NEG_INF)) for h in hs]
    egc = [jnp.exp(x) for x in gc]
    s = [s_ref[0, h] for h in hs]
    s_bf = [x.astype(BF16) for x in s]
    kk = [_dot_nt(k_bf[h], k_bf[h]) for h in hs]
    k_s = [_dot(k_bf[h], s_bf[h]) for h in hs]
    q_s = [_dot(q_bf[h], s_bf[h]) for h in hs]
    qk = [(_dot_nt(q_bf[h], k_bf[h]) * decay[h]).astype(BF16) for h in hs]
    power = [jnp.where(strict, beta[h] * decay[h] * kk[h], 0.0).astype(BF16) for h in hs]
    u = [beta[h] * (v[h] - egc[h] * k_s[h]) for h in hs]
    powers = [power]
    span = 2
    while span < chunk:
        powers.append([_dot(p, p).astype(BF16) for p in powers[-1]])
        span *= 2
    for power in reversed(powers[1:]):
        u = [u[h] + _dot(power[h], u[h].astype(BF16)) for h in hs]
    u = [u[h] - _dot(powers[0][h], u[h].astype(BF16)) for h in hs]
    u_bf = [x.astype(BF16) for x in u]
    o = [egc[h] * q_s[h] + _dot(qk[h], u_bf[h]) for h in hs]
    glast = [x[chunk - 1:chunk, :] for x in gc]
    kd = [(k[h] * jnp.exp(glast[h] - gc[h])).astype(BF16) for h in hs]
    for h in hs:
        s_ref[0, h] = jnp.exp(glast[h]) * s[h] + _dot_tn(kd[h], u_bf[h])
    for h in hs:
        zg = _silu(z_ref[0, :, h * dv:(h + 1) * dv])
        on = o[h] * lax.rsqrt(jnp.mean(o[h] * o[h], axis=-1, keepdims=True) + RMS_EPS)
        og_ref[0, :, h * dv:(h + 1) * dv] = on * ng_ref[...] * zg


def gdn_core(proj, conv_prev, s0, conv_w, a_log, dt_bias, norm_g, *, t_valid, cache_k=None, page_ids=None):
    bsz, tp, _ = proj.shape
    heads = a_log.shape[0]
    dv = norm_g.shape[0]
    qkv_dim = conv_w.shape[1]
    dk = (qkv_dim // heads - dv) // 2
    z_dim = heads * dv
    chunk = GDN_CHUNK
    assert tp % chunk == 0 and dk == LANES and dv == LANES and 2 * heads <= LANES
    assert conv_w.shape[0] - 1 <= CONV_PAD_ROWS <= chunk
    par = jnp.zeros((8, LANES), F32)
    par = par.at[0, heads:2 * heads].set(a_log).at[1, heads:2 * heads].set(dt_bias)
    cprev = jnp.pad(conv_prev, ((0, 0), (CONV_PAD_ROWS - conv_prev.shape[1], 0), (0, 0)))
    nq = qkv_dim // LANES
    n_chunks = tp // chunk
    out_shape = [jax.ShapeDtypeStruct((bsz, tp, z_dim), F32), jax.ShapeDtypeStruct((bsz, heads, dk, dv), F32)]
    out_specs = [pl.BlockSpec((1, chunk, z_dim), lambda b, c, ids: (b, c, 0)),
                 pl.BlockSpec((1, heads, dk, dv), lambda b, c, ids: (b, 0, 0, 0))]
    page_specs, page_args, n_pages_step, ppb = [], [], 0, 1
    if cache_k is None:
        page_ids = jnp.zeros((1,), jnp.int32)
    else:
        _, page, kv_heads, hd = cache_k.shape
        ppb = MOBA_BLOCK // page
        n_pages_step = page_ids.shape[0] // (bsz * n_chunks)
        assert n_pages_step * bsz * n_chunks == page_ids.shape[0] and n_pages_step % ppb == 0

        def page_spec(i):
            return pl.BlockSpec((1, page, kv_heads, hd),
                                lambda b, c, ids: (ids[(b * n_chunks + c) * n_pages_step + i], 0, 0, 0))

        page_specs = [page_spec(i) for i in range(n_pages_step)]
        page_args = [cache_k] * n_pages_step
        out_shape.append(jax.ShapeDtypeStruct((bsz * n_chunks, n_pages_step // ppb, kv_heads, hd), F32))
        out_specs.append(pl.BlockSpec((1, n_pages_step // ppb, kv_heads, hd),
                                      lambda b, c, ids: (b * n_chunks + c, 0, 0, 0)))
    kern = functools.partial(_gdn_kernel, chunk=chunk, heads=heads, dk=dk, dv=dv, t_valid=t_valid,
                             n_pages_step=n_pages_step, ppb=ppb)
    outs = pl.pallas_call(
        kern,
        out_shape=tuple(out_shape),
        grid_spec=pltpu.PrefetchScalarGridSpec(
            num_scalar_prefetch=1,
            grid=(bsz, n_chunks),
            in_specs=[
                pl.BlockSpec((1, chunk, qkv_dim), lambda b, c, ids: (b, c, 0)),
                pl.BlockSpec((1, chunk, z_dim), lambda b, c, ids: (b, c, qkv_dim // z_dim)),
                pl.BlockSpec((1, chunk, LANES), lambda b, c, ids: (b, c, nq + z_dim // LANES)),
                pl.BlockSpec((1, CONV_PAD_ROWS, qkv_dim), lambda b, c, ids: (b, 0, 0)),
                pl.BlockSpec((1, heads, dk, dv), lambda b, c, ids: (b, 0, 0, 0)),
                pl.BlockSpec(conv_w.shape, lambda b, c, ids: (0, 0)),
                pl.BlockSpec((8, LANES), lambda b, c, ids: (0, 0)),
                pl.BlockSpec((1, dv), lambda b, c, ids: (0, 0)),
            ] + page_specs,
            out_specs=tuple(out_specs),
            scratch_shapes=[pltpu.VMEM((CONV_PAD_ROWS + chunk, qkv_dim), F32)],
        ),
        compiler_params=_params("parallel", "arbitrary"),
        name="gdn_core",
    )(page_ids, proj, proj, proj, cprev, s0, conv_w, par, norm_g.reshape(1, dv), *page_args)
    if cache_k is None:
        return outs[0], outs[1], None
    return outs[0], outs[1], outs[2].reshape(-1, kv_heads, hd)


def _moe_route_kernel(x_ref, g_ref, wr_ref, br_ref, pos_ref, wts_ref, seg_ref, *, n_groups, n_experts):
    tm = x_ref.shape[0]
    epg = n_experts // n_groups
    xn = _rms(x_ref[...], g_ref[...])
    logits = jnp.dot(xn, wr_ref[...], precision=HIGHEST, preferred_element_type=F32) + br_ref[...]
    lt = logits.T
    sub = lax.broadcasted_iota(jnp.int32, lt.shape, 0)
    is_g = sub < n_groups
    gl = jnp.where(is_g, lt, NEG_INF)
    ge = jnp.exp(gl - jnp.max(gl, axis=0, keepdims=True))
    pg = ge / jnp.sum(ge, axis=0, keepdims=True)
    pg_sel = jnp.max(pg, axis=0, keepdims=True)
    g_idx = jnp.min(jnp.where(is_g & (pg == pg_sel), sub, LANES), axis=0, keepdims=True)
    lo = n_groups + g_idx * epg
    in_g = (sub >= lo) & (sub < lo + epg)
    el = jnp.where(in_g, lt, NEG_INF)
    ee = jnp.exp(el - jnp.max(el, axis=0, keepdims=True))
    pe = ee / jnp.sum(ee, axis=0, keepdims=True)
    rest = jnp.where(in_g, pe, NEG_INF)
    tops = []
    for _ in range(TOPK_IN_GROUP):
        top_v = jnp.max(rest, axis=0, keepdims=True)
        top_i = jnp.min(jnp.where(rest == top_v, sub, LANES), axis=0, keepdims=True)
        tops.append((top_v, top_i))
        rest = jnp.where(sub == top_i, NEG_INF, rest)
    denom = tops[0][0]
    for top_v, _ in tops[1:]:
        denom = denom + top_v
    wts_ref[0] = jnp.concatenate([pg_sel * top_v / denom for top_v, _ in tops], axis=1)

    onehot = [jnp.where(sub == top_i, 1.0, 0.0) for _, top_i in tops]
    span = min(tm, 512)
    earlier = jnp.where(lax.broadcasted_iota(jnp.int32, (span, span), 0)
                        < lax.broadcasted_iota(jnp.int32, (span, span), 1), 1.0, 0.0).astype(BF16)
    count = [jnp.sum(oh, axis=1, keepdims=True) for oh in onehot]
    total = count[0]
    for c in count[1:]:
        total = total + c
    padded = jnp.floor((total + (SUBLANES - 1)) * (1.0 / SUBLANES)) * SUBLANES
    below = jnp.where(lax.broadcasted_iota(jnp.int32, (LANES, LANES), 1) < lax.broadcasted_iota(jnp.int32, (LANES, LANES), 0),
                      1.0, 0.0)
    start = jnp.dot(below, jnp.broadcast_to(padded, (LANES, LANES)), precision=HIGHEST,
                    preferred_element_type=F32)[:, 0:1]
    pos = []
    base = start
    for oh, c in zip(onehot, count):
        for lo_n in range(0, tm, span):
            oh_n = oh[:, lo_n:lo_n + span]
            before = _dot(oh_n.astype(BF16), earlier)
            pos.append(jnp.sum(oh_n * (base + before), axis=0, keepdims=True))
            base = base + jnp.sum(oh_n, axis=1, keepdims=True)
    pos_ref[0] = jnp.concatenate(pos, axis=1).astype(jnp.int32)
    lane = lax.broadcasted_iota(jnp.int32, (LANES, LANES), 1)
    cols = jnp.where(lane == 0, start, 0.0) + jnp.where(lane == 1, total, 0.0)
    seg_ref[0] = cols.T[0:8, :].astype(jnp.int32)


def _moe_expert_kernel(pos_ref, wts_ref, seg_ref, x_ref, g_ref, g_out_ref, wg_ref, wu_ref, wd_ref, o_ref, rows_ref,
                       *, n_groups, n_experts, n_slots, row_block, norm_out):
    step = pl.program_id(1)
    tm, d = x_ref.shape
    experts_step = wg_ref.shape[0]

    @pl.when(step == 0)
    def _():
        o_ref[...] = _rms(x_ref[...], g_ref[...])
        end = 0
        for j in range(n_experts):
            end = seg_ref[0, 0, n_groups + j] + seg_ref[0, 1, n_groups + j]
            last_tile = pl.multiple_of(jnp.maximum((end - 1) // SUBLANES * SUBLANES, 0), SUBLANES)
            rows_ref[pl.ds(last_tile, SUBLANES), :] = jnp.zeros((SUBLANES, d), F32)
        tail = pl.multiple_of((end + SUBLANES - 1) // SUBLANES * SUBLANES, SUBLANES)
        rows_ref[pl.ds(tail, row_block), :] = jnp.zeros((row_block, d), F32)

        def scatter(n, carry):
            row = o_ref[pl.ds(n, 1), :]
            for k in range(n_slots):
                rows_ref[pl.ds(pos_ref[0, 0, k * tm + n], 1), :] = row
            return carry

        lax.fori_loop(0, tm, scatter, 0, unroll=8)

    for j in range(experts_step):
        lane = n_groups + step * experts_step + j
        start = seg_ref[0, 0, lane]
        count = seg_ref[0, 1, lane]

        def run_block(i, carry, j=j, start=start, count=count):
            r0 = pl.multiple_of(start + i * row_block, SUBLANES)
            xin = rows_ref[pl.ds(r0, row_block), :]
            xb = xin.astype(BF16)
            he = _silu(_dot(xb, wg_ref[j].astype(BF16))) * _dot(xb, wu_ref[j].astype(BF16))
            y = _dot(he.astype(BF16), wd_ref[j].astype(BF16))
            mine = lax.broadcasted_iota(jnp.int32, (row_block, 1), 0) < count - i * row_block
            rows_ref[pl.ds(r0, row_block), :] = jnp.where(mine, y, xin)
            return carry

        lax.fori_loop(0, (count + row_block - 1) // row_block, run_block, 0)

    @pl.when(step == pl.num_programs(1) - 1)
    def _():
        def gather(n, carry):
            acc = x_ref[pl.ds(n, 1), :]
            for k in range(n_slots):
                acc = acc + wts_ref[0, 0, k * tm + n] * rows_ref[pl.ds(pos_ref[0, 0, k * tm + n], 1), :]
            o_ref[pl.ds(n, 1), :] = acc
            return carry

        lax.fori_loop(0, tm, gather, 0, unroll=8)
        if norm_out:
            o_ref[...] = _rms(o_ref[...], g_out_ref[...])


def hier_moe_residual(x, g, w_route, b_route, w_gate, w_up, w_down, *, n_groups, tm, g_out=None):
    n, d = x.shape
    n_experts, _, ff = w_gate.shape
    n_slots = TOPK_IN_GROUP
    assert n % tm == 0 and tm % LANES == 0 and n_groups + n_experts <= LANES
    n_tiles = n // tm
    row_block = min(MOE_ROW_BLOCK, tm)
    rows_cap = -(-(n_slots * tm + n_experts * (SUBLANES - 1)) // SUBLANES) * SUBLANES + row_block
    route =functools.partial(_moe_route_kernel, n_groups=n_groups, n_experts=n_experts)
    pos, wts, seg = pl.pallas_call(
        route,
        out_shape=(jax.ShapeDtypeStruct((n_tiles, 1, n_slots * tm), jnp.int32),
                   jax.ShapeDtypeStruct((n_tiles, 1, n_slots * tm), F32),
                   jax.ShapeDtypeStruct((n_tiles, 8, LANES), jnp.int32)),
        grid=(n_tiles,),
        in_specs=[
            pl.BlockSpec((tm, d), lambda i: (i, 0)),
            pl.BlockSpec((1, d), lambda i: (0, 0)),
            pl.BlockSpec((d, LANES), lambda i: (0, 0)),
            pl.BlockSpec((1, LANES), lambda i: (0, 0)),
        ],
        out_specs=(pl.BlockSpec((1, 1, n_slots * tm), lambda i: (i, 0, 0)),
                   pl.BlockSpec((1, 1, n_slots * tm), lambda i: (i, 0, 0)),
                   pl.BlockSpec((1, 8, LANES), lambda i: (i, 0, 0))),
        compiler_params=_params("parallel"),
        name="moe_route",
    )(x, g.reshape(1, d), w_route, b_route)

    def smem_spec(shape):
        return pl.BlockSpec((1,) + shape, lambda i, e: (i, 0, 0), memory_space=pltpu.SMEM)

    expert = functools.partial(_moe_expert_kernel, n_groups=n_groups, n_experts=n_experts, n_slots=n_slots,
                               row_block=row_block, norm_out=g_out is not None)
    once = pl.Buffered(1)
    eps = MOE_EXPERTS_PER_STEP
    assert n_experts % eps == 0
    return pl.pallas_call(
        expert,
        out_shape=jax.ShapeDtypeStruct((n, d), F32),
        grid=(n_tiles, n_experts // eps),
        in_specs=[
            smem_spec((1, n_slots * tm)),
            smem_spec((1, n_slots * tm)),
            smem_spec((8, LANES)),
            pl.BlockSpec((tm, d), lambda i, e: (i, 0), pipeline_mode=once),
            pl.BlockSpec((1, d), lambda i, e: (0, 0)),
            pl.BlockSpec((1, d), lambda i, e: (0, 0)),
            pl.BlockSpec((eps, d, ff), lambda i, e: (e, 0, 0)),
            pl.BlockSpec((eps, d, ff), lambda i, e: (e, 0, 0)),
            pl.BlockSpec((eps, ff, d), lambda i, e: (e, 0, 0)),
        ],
        out_specs=pl.BlockSpec((tm, d), lambda i, e: (i, 0), pipeline_mode=once),
        scratch_shapes=[pltpu.VMEM((rows_cap, d), F32)],
        compiler_params=_params("parallel", "arbitrary"),
        name="moe_experts",
    )(pos, wts, seg, x, g.reshape(1, d), (g if g_out is None else g_out).reshape(1, d), w_gate, w_up, w_down)


def _moba_prompt_kernel(q_ref, k_ref, v_ref, o_ref, kbf_ref, vt_ref, kmean_ref, sel_ref, lg_ref,
                        *, n_blocks, n_sel, scale):
    qi = pl.program_id(2)
    blk = MOBA_BLOCK
    hd = kbf_ref.shape[2]
    hs = range(q_ref.shape[2] // hd)
    nb_pad = kmean_ref.shape[0] // len(hs)

    @pl.when(qi == 0)
    def _():
        kmean_ref[...] = jnp.zeros_like(kmean_ref)
        for h in hs:
            for j in range(n_blocks):
                kj = k_ref[0, j * blk:(j + 1) * blk, h * hd:(h + 1) * hd]
                kmean_ref[h * nb_pad + j:h * nb_pad + j + 1, :] = jnp.mean(kj, axis=0, keepdims=True)
                kbf_ref[h * n_blocks + j] = kj.astype(BF16)
                vt_ref[h * n_blocks + j] = v_ref[0, j * blk:(j + 1) * blk, h * hd:(h + 1) * hd].T.astype(BF16)

    q = [q_ref[0, :, h * hd:(h + 1) * hd] for h in hs]
    q_bf = [x.astype(BF16) for x in q]
    gate = [_dot_nt(kmean_ref[h * nb_pad:(h + 1) * nb_pad, :], q[h], precision=HIGHEST) for h in hs]
    blk_i = lax.broadcasted_iota(jnp.int32, gate[0].shape, 0)
    past = blk_i < qi
    for h in hs:
        for j in range(n_blocks):
            gj = gate[h][j:j + 1, :]
            ahead = ((gate[h] > gj) | ((gate[h] == gj) & (blk_i < j))) & past
            rank = jnp.sum(jnp.where(ahead, 1.0, 0.0), axis=0, keepdims=True)
            sel_ref[h * nb_pad + j:h * nb_pad + j + 1, :] = jnp.where(rank < n_sel, 1.0, 0.0)

    def past_block(j, tops):
        new = []
        for h in hs:
            sc = _dot_nt(kbf_ref[h * n_blocks + j], q_bf[h]) * scale
            sc = jnp.where(sel_ref[pl.ds(h * nb_pad + j, 1), :] > 0.5, sc, NEG_INF)
            lg_ref[h * n_blocks + j] = sc
            new.append(jnp.maximum(tops[h], jnp.max(sc, axis=0, keepdims=True)))
        return tuple(new)

    tops = lax.fori_loop(0, qi, past_block, tuple(jnp.full((1, blk), NEG_INF, F32) for _ in hs))
    key_i = lax.broadcasted_iota(jnp.int32, (blk, blk), 0)
    qry_i = lax.broadcasted_iota(jnp.int32, (blk, blk), 1)
    top = []
    for h in hs:
        own = jnp.where(key_i <= qry_i, _dot_nt(kbf_ref[h * n_blocks + qi], q_bf[h]) * scale, NEG_INF)
        lg_ref[h * n_blocks + qi] = own
        top.append(jnp.maximum(tops[h], jnp.max(own, axis=0, keepdims=True)))

    def weigh_block(j, carry):
        new = []
        for h in hs:
            den, out = carry[h]
            p = jnp.exp(lg_ref[h * n_blocks + j] - top[h])
            new.append((den + jnp.sum(p, axis=0, keepdims=True), out + _dot(vt_ref[h * n_blocks + j], p.astype(BF16))))
        return tuple(new)

    acc = lax.fori_loop(0, qi + 1, weigh_block,
                        tuple((jnp.zeros((1, blk), F32), jnp.zeros((hd, blk), F32)) for _ in hs))
    for h in hs:
        den, out = acc[h]
        o_ref[0, :, h * hd:(h + 1) * hd] = (out / den).T


def moba_prompt(q, k, v, *, heads):
    bsz, s, hd_all = q.shape
    hd = hd_all // heads
    assert s % MOBA_BLOCK == 0 and hd == LANES
    n_blocks = s // MOBA_BLOCK
    nb_pad = -(-n_blocks // SUBLANES) * SUBLANES
    n_sel = min(MOBA_TOPK, (s - 1) // MOBA_BLOCK)
    hps = min(heads, MOBA_HEADS_PER_STEP)
    assert heads % hps == 0
    kern = functools.partial(_moba_prompt_kernel, n_blocks=n_blocks, n_sel=n_sel, scale=hd ** -0.5)
    return pl.pallas_call(
        kern,
        out_shape=jax.ShapeDtypeStruct((bsz, s, hd_all), F32),
        grid=(bsz, heads // hps, n_blocks),
        in_specs=[
            pl.BlockSpec((1, MOBA_BLOCK, hps * hd), lambda b, h, i: (b, i, h)),
            pl.BlockSpec((1, s, hps * hd), lambda b, h, i: (b, 0, h)),
            pl.BlockSpec((1, s, hps * hd), lambda b, h, i: (b, 0, h)),
        ],
        out_specs=pl.BlockSpec((1, MOBA_BLOCK, hps * hd), lambda b, h, i: (b, i, h)),
        scratch_shapes=[
            pltpu.VMEM((hps * n_blocks, MOBA_BLOCK, hd), BF16),
            pltpu.VMEM((hps * n_blocks, hd, MOBA_BLOCK), BF16),
            pltpu.VMEM((hps * nb_pad, hd), F32),
            pltpu.VMEM((hps * nb_pad, MOBA_BLOCK), F32),
            pltpu.VMEM((hps * n_blocks, MOBA_BLOCK, MOBA_BLOCK), F32),
        ],
        compiler_params=_params("parallel", "parallel", "arbitrary"),
        name="moba_prompt",
    )(q, k, v)


def _sample_select_kernel(q_ref, bm_ref, pt_ref, o_ref, *, heads, n_sel, ppb):
    rows, hd = q_ref.shape[1], q_ref.shape[2]
    nb = bm_ref.shape[2]
    n_pages = pt_ref.shape[2]
    q = q_ref[0]
    row_head = lax.broadcasted_iota(jnp.int32, (rows, nb), 0) % heads
    lane = lax.broadcasted_iota(jnp.int32, (rows, nb), 1)
    lane_pt = lax.broadcasted_iota(jnp.int32, (rows, n_pages), 1)
    lane_out = lax.broadcasted_iota(jnp.int32, (rows, LANES), 1)
    gate = jnp.zeros((rows, nb), F32)
    for h in range(heads):
        gate_h = _dot_nt(q, bm_ref[0, h], precision=HIGHEST)
        gate = jnp.where(row_head == h, gate_h, gate)
    pt = pt_ref[0].astype(F32)
    out = jnp.zeros((rows, LANES), F32)
    for s in range(n_sel):
        top = jnp.max(gate, axis=-1, keepdims=True)
        idx = jnp.min(jnp.where(gate == top, lane, nb), axis=-1, keepdims=True)
        gate = jnp.where(lane == idx, NEG_INF, gate)
        for p in range(ppb):
            phys = jnp.sum(jnp.where(lane_pt == idx * ppb + p, pt, 0.0), axis=-1, keepdims=True)
            out = out + jnp.where(lane_out == s * ppb + p, phys, 0.0)
    o_ref[0] = out.astype(jnp.int32)


def sample_select(q, bmean, page_table, *, heads, n_sel, ppb):
    dbsz, rows, hd = q.shape
    nb = bmean.shape[2]
    n_pages = page_table.shape[1]
    kern = functools.partial(_sample_select_kernel, heads=heads, n_sel=n_sel, ppb=ppb)
    return pl.pallas_call(
        kern,
        out_shape=jax.ShapeDtypeStruct((dbsz, rows, LANES), jnp.int32),
        grid=(dbsz,),
        in_specs=[
            pl.BlockSpec((1, rows, hd), lambda b: (b, 0, 0)),
            pl.BlockSpec((1, heads, nb, hd), lambda b: (b, 0, 0, 0)),
            pl.BlockSpec((1, 1, n_pages), lambda b: (b, 0, 0)),
        ],
        out_specs=pl.BlockSpec((1, rows, LANES), lambda b: (b, 0, 0)),
        compiler_params=_params("parallel"),
        name="sample_select",
    )(q, bmean, page_table.reshape(dbsz, 1, n_pages))


def _sample_attend_kernel(phys_ref, q_ref, kn_ref, vn_ref, ck_ref, cv_ref, o_ref, kbuf_ref, vbuf_ref, sem_ref,
                          *, n_slabs, t_len, heads, scale):
    n_gather = t_len * n_slabs
    step = pl.program_id(0)
    n_steps = pl.num_programs(0)
    slot = step % 2

    def slab_copies(of_step, to_slot):
        b, h = of_step // heads, of_step % heads
        copies = []
        for t in range(t_len):
            for s in range(n_slabs):
                g = t * n_slabs + s
                phys = phys_ref[((b * t_len + t) * heads + h) * n_slabs + s]
                copies.append(pltpu.make_async_copy(ck_ref.at[phys, :, h, :], kbuf_ref.at[to_slot, g],
                                                    sem_ref.at[to_slot, g]))
                copies.append(pltpu.make_async_copy(cv_ref.at[phys, :, h, :], vbuf_ref.at[to_slot, g],
                                                    sem_ref.at[to_slot, n_gather + g]))
        return copies

    @pl.when(step == 0)
    def _():
        for cp in slab_copies(step, slot):
            cp.start()

    @pl.when(step + 1 < n_steps)
    def _():
        for cp in slab_copies(step + 1, 1 - slot):
            cp.start()

    for cp in slab_copies(step, slot):
        cp.wait()
    k_refs = [kbuf_ref.at[slot, g] for g in range(n_gather)]
    v_refs = [vbuf_ref.at[slot, g] for g in range(n_gather)]

    q = q_ref[0, 0]
    rows = q.shape[0]
    q_bf = q.astype(BF16)
    k_new, v_new = kn_ref[0, 0], vn_ref[0, 0]
    row_c = lax.broadcasted_iota(jnp.int32, (rows, 1), 0)
    own = []
    for j in range(t_len):
        lg = jnp.sum(q * k_new[j:j + 1, :], axis=-1, keepdims=True) * scale
        own.append(jnp.where(row_c >= j, lg, NEG_INF))
    page = kbuf_ref.shape[2]
    row_p = lax.broadcasted_iota(jnp.int32, (rows, page), 0)
    past = []
    for s in range(n_slabs):
        lg = None
        for t in range(t_len):
            lg_t = _dot_nt(q_bf, k_refs[t * n_slabs + s][...].astype(BF16)) * scale
            lg = lg_t if lg is None else jnp.where(row_p == t, lg_t, lg)
        past.append(lg)
    top = own[0]
    for lg in own[1:]:
        top = jnp.maximum(top, lg)
    for lg in past:
        top = jnp.maximum(top, jnp.max(lg, axis=-1, keepdims=True))
    den = jnp.zeros((rows, 1), F32)
    out = jnp.zeros(q.shape, F32)
    for j in range(t_len):
        p = jnp.exp(own[j] - top)
        den = den + p
        out = out + p * v_new[j:j + 1, :]
    row_o = lax.broadcasted_iota(jnp.int32, q.shape, 0)
    for s in range(n_slabs):
        p = jnp.exp(past[s] - top)
        den = den + jnp.sum(p, axis=-1, keepdims=True)
        p_bf = p.astype(BF16)
        for t in range(t_len):
            o_t = _dot(p_bf, v_refs[t * n_slabs + s][...].astype(BF16))
            out = out + jnp.where(row_o == t, o_t, 0.0)
    o_ref[0, 0] = out / den


def sample_attend(q, k_new, v_new, cache_k, cache_v, phys, *, t_len, n_slabs):
    dbsz, heads, rows, hd = q.shape
    page = cache_k.shape[1]
    kern = functools.partial(_sample_attend_kernel, n_slabs=n_slabs, t_len=t_len, heads=heads, scale=hd ** -0.5)
    n_gather = t_len * n_slabs
    new_spec = pl.BlockSpec((1, 1, rows, hd), lambda i, ph: (i // heads, i % heads, 0, 0))
    any_spec = pl.BlockSpec(memory_space=pl.ANY)
    return pl.pallas_call(
        kern,
        out_shape=jax.ShapeDtypeStruct((dbsz, heads, rows, hd), F32),
        grid_spec=pltpu.PrefetchScalarGridSpec(
            num_scalar_prefetch=1,
            grid=(dbsz * heads,),
            in_specs=[new_spec, new_spec, new_spec, any_spec, any_spec],
            out_specs=new_spec,
            scratch_shapes=[
                pltpu.VMEM((2, n_gather, page, hd), F32),
                pltpu.VMEM((2, n_gather, page, hd), F32),
                pltpu.SemaphoreType.DMA((2, 2 * n_gather)),
            ],
        ),
        compiler_params=_params("arbitrary"),
        name="sample_attend",
    )(phys, q, k_new, v_new, cache_k, cache_v)


def _row_tile(n, target):
    tm = min(n, target)
    assert n % tm == 0
    return tm


def kernel(x_prompt, x_sample, state_gdn, state_conv, cache_k, cache_v, page_table, norm_mix, norm_ffn,
           gdn_w_in, gdn_conv_w, gdn_a_log, gdn_dt_bias, gdn_norm, gdn_w_out, kv_norm, w_kv, moba_w_q,
           moba_w_o, moe_w_group, moe_b_group, moe_w_router, moe_b_router, moe_w_gate, moe_w_up,
           moe_w_down, norm_final):
    bsz, s, d = x_prompt.shape
    dbsz, t, _ = x_sample.shape
    depth = norm_mix.shape[0]
    n_a = gdn_w_in.shape[0]
    conv_w, qkv_dim = gdn_conv_w.shape[1:]
    gdn_heads, dv = gdn_a_log.shape[1], gdn_norm.shape[1]
    z_dim = gdn_heads * dv
    n_pool, page, moba_heads, hd = cache_k.shape
    attn_dim = moba_heads * hd
    n_pages = page_table.shape[1]
    ppb = MOBA_BLOCK // page
    n_full = n_pages // ppb
    n_groups = moe_w_group.shape[-1]
    n_experts = moe_w_router.shape[-1]
    assert n_pages == n_full * ppb, "a partial past block (tail pages) is not supported"
    assert n_full >= MOBA_TOPK and t >= conv_w - 1 and s >= conv_w - 1 and t <= page

    groups = [(x_prompt.reshape(bsz * s, d), bsz, s), (x_sample.reshape(dbsz * t, d), dbsz, t)]
    tiles = [_row_tile(bsz * s, ROW_TILE), _row_tile(dbsz * t, ROW_TILE)]

    def moe(l, x, tm):
        w_route = jnp.zeros((d, LANES), F32).at[:, :n_groups].set(moe_w_group[l])
        w_route = w_route.at[:, n_groups:n_groups + n_experts].set(moe_w_router[l])
        b_route = jnp.zeros((1, LANES), F32).at[0, :n_groups].set(moe_b_group[l])
        b_route = b_route.at[0, n_groups:n_groups + n_experts].set(moe_b_router[l])
        return hier_moe_residual(x, norm_ffn[l], w_route, b_route, moe_w_gate[l], moe_w_up[l], moe_w_down[l],
                                 n_groups=n_groups, tm=tm, g_out=norm_final if l == depth - 1 else None)

    xs = [g[0] for g in groups]
    gdn_states, conv_states, kvs = [[], []], [[], []], [None, None]
    for l in range(depth):
        if l < n_a:
            in_dim = gdn_w_in.shape[2]
            in_pad = -(-in_dim // GDN_PROJ_COL_TILE) * GDN_PROJ_COL_TILE
            assert qkv_dim + z_dim + LANES <= in_pad and in_dim - qkv_dim - z_dim == 2 * gdn_heads
            w_in = jnp.pad(gdn_w_in[l], ((0, 0), (0, in_pad - in_dim))).astype(BF16)
            w_out = gdn_w_out[l].astype(BF16)
            for gi, (_, nb, tl) in enumerate(groups):
                tm = tiles[gi]
                proj = norm_matmul(xs[gi], norm_mix[l], w_in, tm=tm, tn=GDN_PROJ_COL_TILE).reshape(nb, tl, in_pad)
                tp = -(-tl // GDN_CHUNK) * GDN_CHUNK
                proj_p = jnp.pad(proj, ((0, 0), (0, tp - tl), (0, 0))) if tp != tl else proj
                if gi == 0:
                    conv0 = jnp.zeros((nb, conv_w - 1, qkv_dim), F32)
                    s0 = jnp.zeros((nb, gdn_heads, (qkv_dim // gdn_heads - dv) // 2, dv), F32)
                else:
                    conv0, s0 = state_conv[l], state_gdn[l]
                with_means = gi == 0 and l == 0
                og, s_fin, means = gdn_core(proj_p, conv0, s0, gdn_conv_w[l], gdn_a_log[l], gdn_dt_bias[l],
                                            gdn_norm[l], t_valid=tl, cache_k=cache_k if with_means else None,
                                            page_ids=page_table.reshape(-1) if with_means else None)
                if with_means:
                    bmean = means.reshape(dbsz, n_full, moba_heads, hd).transpose(0, 2, 1, 3)
                og = og[:, :tl].reshape(nb * tl, z_dim)
                gdn_states[gi].append(s_fin)
                conv_states[gi].append(proj[:, tl - (conv_w - 1):, :qkv_dim])
                xs[gi] = matmul_residual(og, w_out, xs[gi], tm=tm, tn=COL_TILE)
        else:
            j = l - n_a
            w_q = moba_w_q[j].astype(BF16)
            w_o = moba_w_o[j].astype(BF16)
            q = norm_matmul(xs[0], norm_mix[l], w_q, tm=tiles[0], tn=COL_TILE).reshape(bsz, s, attn_dim)
            k_p, v_p = kvs[0]
            o = moba_prompt(q, k_p.reshape(bsz, s, attn_dim), v_p.reshape(bsz, s, attn_dim), heads=moba_heads)
            xs[0] = matmul_residual(o.reshape(bsz * s, attn_dim), w_o, xs[0], tm=tiles[0], tn=COL_TILE)
            q = norm_matmul(xs[1], norm_mix[l], w_q, tm=tiles[1], tn=COL_TILE)
            k_s, v_s = kvs[1]
            n_sel = min(MOBA_TOPK, n_full)
            phys = sample_select(q.reshape(dbsz, t * moba_heads, hd), bmean, page_table,
                                 heads=moba_heads, n_sel=n_sel, ppb=ppb)
            phys = phys[:, :, :n_sel * ppb].reshape(-1)

            def head_major(a):
                a = a.reshape(dbsz, t, moba_heads, hd).transpose(0, 2, 1, 3)
                return jnp.pad(a, ((0, 0), (0, 0), (0, -t % 8), (0, 0)))

            o = sample_attend(head_major(q), head_major(k_s), head_major(v_s), cache_k, cache_v, phys,
                              t_len=t, n_slabs=n_sel * ppb)
            o = o[:, :, :t].transpose(0, 2, 1, 3).reshape(dbsz * t, attn_dim)
            xs[1] = matmul_residual(o, w_o, xs[1], tm=tiles[1], tn=COL_TILE)
        for gi in range(2):
            xs[gi] = moe(l, xs[gi], _row_tile(xs[gi].shape[0], MOE_TOKEN_TILE))
        if l == n_a - 1:
            w_kv_bf = w_kv.astype(BF16)
            for gi, (_, nb, tl) in enumerate(groups):
                kv = norm_matmul(xs[gi], kv_norm, w_kv_bf, tm=tiles[gi], tn=COL_TILE)
                kvs[gi] = (kv[:, :attn_dim].reshape(nb, tl, moba_heads, hd),
                           kv[:, attn_dim:].reshape(nb, tl, moba_heads, hd))
    y_p = xs[0].reshape(bsz, s, d)
    y_s = xs[1].reshape(dbsz, t, d)
    return (y_p, y_s, jnp.stack(gdn_states[0]), jnp.stack(conv_states[0]), kvs[0][0], kvs[0][1],
            jnp.stack(gdn_states[1]), jnp.stack(conv_states[1]), kvs[1][0], kvs[1][1])
```

```python
import functools

import jax
import jax.numpy as jnp
from jax import lax
from jax.experimental import pallas as pl
from jax.experimental.pallas import tpu as pltpu

RMS_EPS = 1e-6
L2_EPS = 1e-6
MOBA_BLOCK = 256
MOBA_TOPK = 3
MOBA_HEADS_PER_STEP = 4
TOPK_IN_GROUP = 2
GDN_CHUNK = 64
LANES = 128
SUBLANES = 8
ROW_TILE = 1024
COL_TILE = 1024
GDN_PROJ_COL_TILE = 11 * LANES
MOE_TOKEN_TILE = 2048
MOE_ROW_BLOCK = 256
MOE_EXPERTS_PER_STEP = 2
CONV_PAD_ROWS = SUBLANES
VMEM_LIMIT = 56 * 1024 * 1024

F32 = jnp.float32
BF16 = jnp.bfloat16
HIGHEST = lax.Precision.HIGHEST
NEG_INF = float("-inf")


def _params(*sem):
    return pltpu.CompilerParams(dimension_semantics=sem, vmem_limit_bytes=VMEM_LIMIT)


def _rms(x, g):
    return x * lax.rsqrt(jnp.mean(x * x, axis=-1, keepdims=True) + RMS_EPS) * g


def _silu(x):
    return x * jax.nn.sigmoid(x)


def _softplus(x):
    return jnp.maximum(x, 0.0) + jnp.log1p(jnp.exp(-jnp.abs(x)))


def _dot(a, b):
    return jnp.dot(a, b, preferred_element_type=F32)


def _dot_nt(a, b, precision=None):
    return lax.dot_general(a, b, (((1,), (1,)), ((), ())), precision=precision, preferred_element_type=F32)


def _dot_tn(a, b):
    return lax.dot_general(a, b, (((0,), (0,)), ((), ())), preferred_element_type=F32)


def _norm_matmul_kernel(x_ref, g_ref, w_ref, o_ref, xn_ref):
    @pl.when(pl.program_id(1) == 0)
    def _():
        xn_ref[...] = _rms(x_ref[...], g_ref[...]).astype(BF16)

    o_ref[...] = _dot(xn_ref[...], w_ref[...])


def norm_matmul(x, g, w, *, tm, tn):
    n, d = x.shape
    m = w.shape[1]
    assert n % tm == 0 and m % tn == 0
    return pl.pallas_call(
        _norm_matmul_kernel,
        out_shape=jax.ShapeDtypeStruct((n, m), F32),
        grid=(n // tm, m // tn),
        in_specs=[
            pl.BlockSpec((tm, d), lambda i, j: (i, 0)),
            pl.BlockSpec((1, d), lambda i, j: (0, 0)),
            pl.BlockSpec((d, tn), lambda i, j: (0, j)),
        ],
        out_specs=pl.BlockSpec((tm, tn), lambda i, j: (i, j)),
        scratch_shapes=[pltpu.VMEM((tm, d), BF16)],
        compiler_params=_params("parallel", "arbitrary"),
        name="norm_matmul",
    )(x, g.reshape(1, d), w)


def _matmul_residual_kernel(a_ref, w_ref, r_ref, o_ref):
    o_ref[...] = r_ref[...] + _dot(a_ref[...].astype(BF16), w_ref[...])


def matmul_residual(a, w, resid, *, tm, tn):
    n, k = a.shape
    m = w.shape[1]
    assert n % tm == 0 and m % tn == 0
    return pl.pallas_call(
        _matmul_residual_kernel,
        out_shape=jax.ShapeDtypeStruct((n, m), F32),
        grid=(n // tm, m // tn),
        in_specs=[
            pl.BlockSpec((tm, k), lambda i, j: (i, 0)),
            pl.BlockSpec((k, tn), lambda i, j: (0, j)),
            pl.BlockSpec((tm, tn), lambda i, j: (i, j)),
        ],
        out_specs=pl.BlockSpec((tm, tn), lambda i, j: (i, j)),
        compiler_params=_params("parallel", "parallel"),
        name="matmul_residual",
    )(a, w, resid)


def _gdn_kernel(page_ids_ref, qkv_ref, z_ref, ba_ref, cprev_ref, s0_ref, cw_ref, par_ref, ng_ref, *refs,
                chunk, heads, dk, dv, t_valid, n_pages_step, ppb):
    del page_ids_ref
    page_refs = refs[:n_pages_step]
    if n_pages_step:
        og_ref, s_ref, bm_ref, xbuf_ref = refs[n_pages_step:]
        for n in range(n_pages_step // ppb):
            acc = None
            for p in range(ppb):
                part = jnp.sum(page_refs[n * ppb + p][0], axis=0)
                acc = part if acc is None else acc + part
            bm_ref[0, n] = acc / MOBA_BLOCK
    else:
        og_ref, s_ref, xbuf_ref = refs
    c = pl.program_id(1)
    hk = heads * dk
    hist = CONV_PAD_ROWS
    conv_w = cw_ref.shape[0]

    @pl.when(c == 0)
    def _():
        xbuf_ref[0:hist, :] = cprev_ref[0]
        s_ref[...] = s0_ref[...]

    xbuf_ref[hist:hist + chunk, :] = qkv_ref[0]

    row = lax.broadcasted_iota(jnp.int32, (chunk, LANES), 0)
    valid = (row + c * chunk) < t_valid
    ba = ba_ref[0]
    neg_a = -jnp.exp(par_ref[0:1, :])
    dt_bias = par_ref[1:2, :]
    beta_all = jnp.where(valid, jax.nn.sigmoid(ba), 0.0)
    g_all = jnp.where(valid, neg_a * _softplus(ba + dt_bias), 0.0)

    r_i = lax.broadcasted_iota(jnp.int32, (chunk, chunk), 0)
    c_i = lax.broadcasted_iota(jnp.int32, (chunk, chunk), 1)
    incl = c_i <= r_i
    strict = c_i < r_i
    gcum_all = jnp.dot(incl.astype(F32), g_all, precision=HIGHEST, preferred_element_type=F32)

    def conv_silu(col, width):
        acc = None
        for i in range(conv_w):
            lo = hist - (conv_w - 1) + i
            term = xbuf_ref[lo:lo + chunk, col:col + width] * cw_ref[i:i + 1, col:col + width]
            acc = term if acc is None else acc + term
        return _silu(acc)

    def l2n(x):
        return x * lax.rsqrt(jnp.sum(x * x, axis=-1, keepdims=True) + L2_EPS)

    gpad =jnp.concatenate([gcum_all, jnp.zeros((LANES - chunk, LANES), F32)], axis=0) if chunk < LANES else gcum_all
    gcum_t = gpad.T

    hs = range(heads)
    q = [l2n(conv_silu(h * dk, dk)) * (dk ** -0.5) for h in hs]
    k = [l2n(conv_silu(hk + h * dk, dk)) for h in hs]
    v = [conv_silu(2 * hk + h * dv, dv) for h in hs]
    xbuf_ref[0:hist, :] = xbuf_ref[chunk:chunk + hist, :]
    q_bf = [x.astype(BF16) for x in q]
    k_bf = [x.astype(BF16) for x in k]
    beta = [beta_all[:, h:h + 1] for h in hs]
    gc = [gcum_all[:, heads + h:heads + h + 1] for h in hs]
    decay = [jnp.exp(jnp.where(incl, gc[h] - gcum_t[heads + h:heads + h + 1, :chunk], NEG_INF)) for h in hs]
    egc = [jnp.exp(x) for x in gc]
    s = [s_ref[0, h] for h in hs]
    s_bf = [x.astype(BF16) for x in s]
    kk = [_dot_nt(k_bf[h], k_bf[h]) for h in hs]
    k_s = [_dot(k_bf[h], s_bf[h]) for h in hs]
    q_s = [_dot(q_bf[h], s_bf[h]) for h in hs]
    qk = [(_dot_nt(q_bf[h], k_bf[h]) * decay[h]).astype(BF16) for h in hs]
    power = [jnp.where(strict, beta[h] * decay[h] * kk[h], 0.0).astype(BF16) for h in hs]
    u = [beta[h] * (v[h] - egc[h] * k_s[h]) for h in hs]
    powers = [power]
    span = 2
    while span < chunk:
        powers.append([_dot(p, p).astype(BF16) for p in powers[-1]])
        span *= 2
    for power in reversed(powers[1:]):
        u = [u[h] + _dot(power[h], u[h].astype(BF16)) for h in hs]
    u = [u[h] - _dot(powers[0][h], u[h].astype(BF16)) for h in hs]
    u_bf = [x.astype(BF16) for x in u]
    o = [egc[h] * q_s[h] + _dot(qk[h], u_bf[h]) for h in hs]
    glast = [x[chunk - 1:chunk, :] for x in gc]
    kd = [(k[h] * jnp.exp(glast[h] - gc[h])).astype(BF16) for h in hs]
    for h in hs:
        s_ref[0, h] = jnp.exp(glast[h]) * s[h] + _dot_tn(kd[h], u_bf[h])
    for h in hs:
        zg = _silu(z_ref[0, :, h * dv:(h + 1) * dv])
        on = o[h] * lax.rsqrt(jnp.mean(o[h] * o[h], axis=-1, keepdims=True) + RMS_EPS)
        og_ref[0, :, h * dv:(h + 1) * dv] = on * ng_ref[...] * zg


def gdn_core(proj, conv_prev, s0, conv_w, a_log, dt_bias, norm_g, *, t_valid, cache_k=None, page_ids=None):
    bsz, tp, _ = proj.shape
    heads = a_log.shape[0]
    dv = norm_g.shape[0]
    qkv_dim = conv_w.shape[1]
    dk = (qkv_dim // heads - dv) // 2
    z_dim = heads * dv
    chunk = GDN_CHUNK
    assert tp % chunk == 0 and dk == LANES and dv == LANES and 2 * heads <= LANES
    assert conv_w.shape[0] - 1 <= CONV_PAD_ROWS <= chunk
    par = jnp.zeros((8, LANES), F32)
    par = par.at[0, heads:2 * heads].set(a_log).at[1, heads:2 * heads].set(dt_bias)
    cprev = jnp.pad(conv_prev, ((0, 0), (CONV_PAD_ROWS - conv_prev.shape[1], 0), (0, 0)))
    nq = qkv_dim // LANES
    n_chunks = tp // chunk
    out_shape = [jax.ShapeDtypeStruct((bsz, tp, z_dim), F32), jax.ShapeDtypeStruct((bsz, heads, dk, dv), F32)]
    out_specs = [pl.BlockSpec((1, chunk, z_dim), lambda b, c, ids: (b, c, 0)),
                 pl.BlockSpec((1, heads, dk, dv), lambda b, c, ids: (b, 0, 0, 0))]
    page_specs, page_args, n_pages_step, ppb = [], [], 0, 1
    if cache_k is None:
        page_ids = jnp.zeros((1,), jnp.int32)
    else:
        _, page, kv_heads, hd = cache_k.shape
        ppb = MOBA_BLOCK // page
        n_pages_step = page_ids.shape[0] // (bsz * n_chunks)
        assert n_pages_step * bsz * n_chunks == page_ids.shape[0] and n_pages_step % ppb == 0

        def page_spec(i):
            return pl.BlockSpec((1, page, kv_heads, hd),
                                lambda b, c, ids: (ids[(b * n_chunks + c) * n_pages_step + i], 0, 0, 0))

        page_specs = [page_spec(i) for i in range(n_pages_step)]
        page_args = [cache_k] * n_pages_step
        out_shape.append(jax.ShapeDtypeStruct((bsz * n_chunks, n_pages_step // ppb, kv_heads, hd), F32))
        out_specs.append(pl.BlockSpec((1, n_pages_step // ppb, kv_heads, hd),
                                      lambda b, c, ids: (b * n_chunks + c, 0, 0, 0)))
    kern = functools.partial(_gdn_kernel, chunk=chunk, heads=heads, dk=dk, dv=dv, t_valid=t_valid,
                             n_pages_step=n_pages_step, ppb=ppb)
    outs = pl.pallas_call(
        kern,
        out_shape=tuple(out_shape),
        grid_spec=pltpu.PrefetchScalarGridSpec(
            num_scalar_prefetch=1,
            grid=(bsz, n_chunks),
            in_specs=[
                pl.BlockSpec((1, chunk, qkv_dim), lambda b, c, ids: (b, c, 0)),
                pl.BlockSpec((1, chunk, z_dim), lambda b, c, ids: (b, c, qkv_dim // z_dim)),
                pl.BlockSpec((1, chunk, LANES), lambda b, c, ids: (b, c, nq + z_dim // LANES)),
                pl.BlockSpec((1, CONV_PAD_ROWS, qkv_dim), lambda b, c, ids: (b, 0, 0)),
                pl.BlockSpec((1, heads, dk, dv), lambda b, c, ids: (b, 0, 0, 0)),
                pl.BlockSpec(conv_w.shape, lambda b, c, ids: (0, 0)),
                pl.BlockSpec((8, LANES), lambda b, c, ids: (0, 0)),
                pl.BlockSpec((1, dv), lambda b, c, ids: (0, 0)),
            ] + page_specs,
            out_specs=tuple(out_specs),
            scratch_shapes=[pltpu.VMEM((CONV_PAD_ROWS + chunk, qkv_dim), F32)],
        ),
        compiler_params=_params("parallel", "arbitrary"),
        name="gdn_core",
    )(page_ids, proj, proj, proj, cprev, s0, conv_w, par, norm_g.reshape(1, dv), *page_args)
    if cache_k is None:
        return outs[0], outs[1], None
    return outs[0], outs[1], outs[2].reshape(-1, kv_heads, hd)


def _moe_route_kernel(x_ref, g_ref, wr_ref, br_ref, pos_ref, wts_ref, seg_ref, *, n_groups, n_experts):
    tm = x_ref.shape[0]
    epg = n_experts // n_groups
    xn = _rms(x_ref[...], g_ref[...])
    logits = jnp.dot(xn, wr_ref[...], precision=HIGHEST, preferred_element_type=F32) + br_ref[...]
    lt = logits.T
    sub = lax.broadcasted_iota(jnp.int32, lt.shape, 0)
    is_g = sub < n_groups
    gl = jnp.where(is_g, lt, NEG_INF)
    ge = jnp.exp(gl - jnp.max(gl, axis=0, keepdims=True))
    pg = ge / jnp.sum(ge, axis=0, keepdims=True)
    pg_sel = jnp.max(pg, axis=0, keepdims=True)
    g_idx = jnp.min(jnp.where(is_g & (pg == pg_sel), sub, LANES), axis=0, keepdims=True)
    lo = n_groups + g_idx * epg
    in_g = (sub >= lo) & (sub < lo + epg)
    el = jnp.where(in_g, lt, NEG_INF)
    ee = jnp.exp(el - jnp.max(el, axis=0, keepdims=True))
    pe = ee / jnp.sum(ee, axis=0, keepdims=True)
    rest = jnp.where(in_g, pe, NEG_INF)
    tops = []
    for _ in range(TOPK_IN_GROUP):
        top_v = jnp.max(rest, axis=0, keepdims=True)
        top_i = jnp.min(jnp.where(rest == top_v, sub, LANES), axis=0, keepdims=True)
        tops.append((top_v, top_i))
        rest = jnp.where(sub == top_i, NEG_INF, rest)
    denom = tops[0][0]
    for top_v, _ in tops[1:]:
        denom = denom + top_v
    wts_ref[0] = jnp.concatenate([pg_sel * top_v / denom for top_v, _ in tops], axis=1)

    onehot = [jnp.where(sub == top_i, 1.0, 0.0) for _, top_i in tops]
    span = min(tm, 512)
    earlier = jnp.where(lax.broadcasted_iota(jnp.int32, (span, span), 0)
                        < lax.broadcasted_iota(jnp.int32, (span, span), 1), 1.0, 0.0).astype(BF16)
    count = [jnp.sum(oh, axis=1, keepdims=True) for oh in onehot]
    total = count[0]
    for c in count[1:]:
        total = total + c
    padded = jnp.floor((total + (SUBLANES - 1)) * (1.0 / SUBLANES)) * SUBLANES
    below = jnp.where(lax.broadcasted_iota(jnp.int32, (LANES, LANES), 1) < lax.broadcasted_iota(jnp.int32, (LANES, LANES), 0),
                      1.0, 0.0)
    start = jnp.dot(below, jnp.broadcast_to(padded, (LANES, LANES)), precision=HIGHEST,
                    preferred_element_type=F32)[:, 0:1]
    pos = []
    base = start
    for oh, c in zip(onehot, count):
        for lo_n in range(0, tm, span):
            oh_n = oh[:, lo_n:lo_n + span]
            before = _dot(oh_n.astype(BF16), earlier)
            pos.append(jnp.sum(oh_n * (base + before), axis=0, keepdims=True))
            base = base + jnp.sum(oh_n, axis=1, keepdims=True)
    pos_ref[0] = jnp.concatenate(pos, axis=1).astype(jnp.int32)
    lane = lax.broadcasted_iota(jnp.int32, (LANES, LANES), 1)
    cols = jnp.where(lane == 0, start, 0.0) + jnp.where(lane == 1, total, 0.0)
    seg_ref[0] = cols.T[0:8, :].astype(jnp.int32)


def _moe_expert_kernel(pos_ref, wts_ref, seg_ref, x_ref, g_ref, g_out_ref, wg_ref, wu_ref, wd_ref, o_ref,
                       rows_ref, stage_ref, *, n_groups, n_experts, n_slots, row_block, norm_out):
    step = pl.program_id(1)
    tm, d = x_ref.shape
    experts_step = wg_ref.shape[0]
    chunk = stage_ref.shape[0]
    tile = stage_ref.shape[1:]

    @pl.when(step == 0)
    def _():
        end = 0
        for j in range(n_experts):
            end = seg_ref[0, 0, n_groups + j] + seg_ref[0, 1, n_groups + j]
            last_rows = jnp.maximum((end - 1) // SUBLANES * SUBLANES, 0)
            rows_ref[pl.ds(last_rows, SUBLANES)] = jnp.zeros((SUBLANES,) + tile, F32)
        tail = (end + SUBLANES - 1) // SUBLANES * SUBLANES
        rows_ref[pl.ds(tail, row_block)] = jnp.zeros((row_block,) + tile, F32)

        for c in range(tm // chunk):
            stage_ref[...] = _rms(x_ref[c * chunk:(c + 1) * chunk, :], g_ref[...]).reshape((chunk,) + tile)

            def scatter(n, carry, c=c):
                row = stage_ref[n]
                for k in range(n_slots):
                    rows_ref[pos_ref[0, 0, k * tm + c * chunk + n]] = row
                return carry

            lax.fori_loop(0, chunk, scatter, 0, unroll=8)

    for j in range(experts_step):
        lane = n_groups + step * experts_step + j
        start = seg_ref[0, 0, lane]
        count = seg_ref[0, 1, lane]

        def run_block(i, carry, j=j, start=start, count=count):
            r0 = start + i * row_block
            xin = rows_ref[pl.ds(r0, row_block)]
            xb = xin.reshape(row_block, d).astype(BF16)
            he = _silu(_dot(xb, wg_ref[j].astype(BF16))) * _dot(xb, wu_ref[j].astype(BF16))
            y = _dot(he.astype(BF16), wd_ref[j].astype(BF16)).reshape(xin.shape)
            mine = lax.broadcasted_iota(jnp.int32, xin.shape, 0) < count - i * row_block
            rows_ref[pl.ds(r0, row_block)] = jnp.where(mine, y, xin)
            return carry

        lax.fori_loop(0, (count + row_block - 1) // row_block, run_block, 0)

    @pl.when(step == pl.num_programs(1) - 1)
    def _():
        for c in range(tm // chunk):
            def gather(n, carry, c=c):
                acc = None
                for k in range(n_slots):
                    i = k * tm + c * chunk + n
                    term = wts_ref[0, 0, i] * rows_ref[pos_ref[0, 0, i]]
                    acc = term if acc is None else acc + term
                stage_ref[n] = acc
                return carry

            lax.fori_loop(0, chunk, gather, 0, unroll=8)
            out = x_ref[c * chunk:(c + 1) * chunk, :] + stage_ref[...].reshape(chunk, d)
            o_ref[c * chunk:(c + 1) * chunk, :] = _rms(out, g_out_ref[...]) if norm_out else out


def hier_moe_residual(x, g, w_route, b_route, w_gate, w_up, w_down, *, n_groups, tm, g_out=None):
    n, d = x.shape
    n_experts, _, ff = w_gate.shape
    n_slots = TOPK_IN_GROUP
    assert n % tm == 0 and tm % LANES == 0 and n_groups + n_experts <= LANES
    n_tiles = n // tm
    row_block = min(MOE_ROW_BLOCK, tm)
    rows_cap = -(-(n_slots * tm + n_experts * (SUBLANES - 1)) // SUBLANES) * SUBLANES + row_block
    route =functools.partial(_moe_route_kernel, n_groups=n_groups, n_experts=n_experts)
    pos, wts, seg = pl.pallas_call(
        route,
        out_shape=(jax.ShapeDtypeStruct((n_tiles, 1, n_slots * tm), jnp.int32),
                   jax.ShapeDtypeStruct((n_tiles, 1, n_slots * tm), F32),
                   jax.ShapeDtypeStruct((n_tiles, 8, LANES), jnp.int32)),
        grid=(n_tiles,),
        in_specs=[
            pl.BlockSpec((tm, d), lambda i: (i, 0)),
            pl.BlockSpec((1, d), lambda i: (0, 0)),
            pl.BlockSpec((d, LANES), lambda i: (0, 0)),
            pl.BlockSpec((1, LANES), lambda i: (0, 0)),
        ],
        out_specs=(pl.BlockSpec((1, 1, n_slots * tm), lambda i: (i, 0, 0)),
                   pl.BlockSpec((1, 1, n_slots * tm), lambda i: (i, 0, 0)),
                   pl.BlockSpec((1, 8, LANES), lambda i: (i, 0, 0))),
        compiler_params=_params("parallel"),
        name="moe_route",
    )(x, g.reshape(1, d), w_route, b_route)

    def smem_spec(shape):
        return pl.BlockSpec((1,) + shape, lambda i, e: (i, 0, 0), memory_space=pltpu.SMEM)

    expert = functools.partial(_moe_expert_kernel, n_groups=n_groups, n_experts=n_experts, n_slots=n_slots,
                               row_block=row_block, norm_out=g_out is not None)
    once = pl.Buffered(1)
    eps = MOE_EXPERTS_PER_STEP
    assert n_experts % eps == 0
    return pl.pallas_call(
        expert,
        out_shape=jax.ShapeDtypeStruct((n, d), F32),
        grid=(n_tiles, n_experts // eps),
        in_specs=[
            smem_spec((1, n_slots * tm)),
            smem_spec((1, n_slots * tm)),
            smem_spec((8, LANES)),
            pl.BlockSpec((tm, d), lambda i, e: (i, 0), pipeline_mode=once),
            pl.BlockSpec((1, d), lambda i, e: (0, 0)),
            pl.BlockSpec((1, d), lambda i, e: (0, 0)),
            pl.BlockSpec((eps, d, ff), lambda i, e: (e, 0, 0)),
            pl.BlockSpec((eps, d, ff), lambda i, e: (e, 0, 0)),
            pl.BlockSpec((eps, ff, d), lambda i, e: (e, 0, 0)),
        ],
        out_specs=pl.BlockSpec((tm, d), lambda i, e: (i, 0), pipeline_mode=once),
        scratch_shapes=[pltpu.VMEM((rows_cap, d // LANES, LANES), F32),
                        pltpu.VMEM((min(tm, MOE_ROW_BLOCK), d // LANES, LANES), F32)],
        compiler_params=_params("parallel", "arbitrary"),
        name="moe_experts",
    )(pos, wts, seg, x, g.reshape(1, d), (g if g_out is None else g_out).reshape(1, d), w_gate, w_up, w_down)


def _moba_prompt_kernel(q_ref, k_ref, v_ref, o_ref, kbf_ref, vt_ref, kmean_ref, sel_ref, lg_ref,
                        *, n_blocks, n_sel, scale):
    qi = pl.program_id(2)
    blk = MOBA_BLOCK
    hd = kbf_ref.shape[2]
    hs = range(q_ref.shape[2] // hd)
    nb_pad = kmean_ref.shape[0] // len(hs)

    @pl.when(qi == 0)
    def _():
        kmean_ref[...] = jnp.zeros_like(kmean_ref)
        for h in hs:
            for j in range(n_blocks):
                kj = k_ref[0, j * blk:(j + 1) * blk, h * hd:(h + 1) * hd]
                kmean_ref[h * nb_pad + j:h * nb_pad + j + 1, :] = jnp.mean(kj, axis=0, keepdims=True)
                kbf_ref[h * n_blocks + j] = kj.astype(BF16)
                vt_ref[h * n_blocks + j] = v_ref[0, j * blk:(j + 1) * blk, h * hd:(h + 1) * hd].T.astype(BF16)

    q = [q_ref[0, :, h * hd:(h + 1) * hd] for h in hs]
    q_bf = [x.astype(BF16) for x in q]
    gate = [_dot_nt(kmean_ref[h * nb_pad:(h + 1) * nb_pad, :], q[h], precision=HIGHEST) for h in hs]
    blk_i = lax.broadcasted_iota(jnp.int32, gate[0].shape, 0)
    past = blk_i < qi
    for h in hs:
        for j in range(n_blocks):
            gj = gate[h][j:j + 1, :]
            ahead = ((gate[h] > gj) | ((gate[h] == gj) & (blk_i < j))) & past
            rank = jnp.sum(jnp.where(ahead, 1.0, 0.0), axis=0, keepdims=True)
            sel_ref[h * nb_pad + j:h * nb_pad + j + 1, :] = jnp.where(rank < n_sel, 1.0, 0.0)

    def past_block(j, tops):
        new = []
        for h in hs:
            sc = _dot_nt(kbf_ref[h * n_blocks + j], q_bf[h]) * scale
            sc = jnp.where(sel_ref[pl.ds(h * nb_pad + j, 1), :] > 0.5, sc, NEG_INF)
            lg_ref[h * n_blocks + j] = sc
            new.append(jnp.maximum(tops[h], jnp.max(sc, axis=0, keepdims=True)))
        return tuple(new)

    tops = lax.fori_loop(0, qi, past_block, tuple(jnp.full((1, blk), NEG_INF, F32) for _ in hs))
    key_i = lax.broadcasted_iota(jnp.int32, (blk, blk), 0)
    qry_i = lax.broadcasted_iota(jnp.int32, (blk, blk), 1)
    top = []
    for h in hs:
        own = jnp.where(key_i <= qry_i, _dot_nt(kbf_ref[h * n_blocks + qi], q_bf[h]) * scale, NEG_INF)
        lg_ref[h * n_blocks + qi] = own
        top.append(jnp.maximum(tops[h], jnp.max(own, axis=0, keepdims=True)))

    def weigh_block(j, carry):
        new = []
        for h in hs:
            den, out = carry[h]
            p = jnp.exp(lg_ref[h * n_blocks + j] - top[h])
            new.append((den + jnp.sum(p, axis=0, keepdims=True), out + _dot(vt_ref[h * n_blocks + j], p.astype(BF16))))
        return tuple(new)

    acc = lax.fori_loop(0, qi + 1, weigh_block,
                        tuple((jnp.zeros((1, blk), F32), jnp.zeros((hd, blk), F32)) for _ in hs))
    for h in hs:
        den, out = acc[h]
        o_ref[0, :, h * hd:(h + 1) * hd] = (out / den).T


def moba_prompt(q, k, v, *, heads):
    bsz, s, hd_all = q.shape
    hd = hd_all // heads
    assert s % MOBA_BLOCK == 0 and hd == LANES
    n_blocks = s // MOBA_BLOCK
    nb_pad = -(-n_blocks // SUBLANES) * SUBLANES
    n_sel = min(MOBA_TOPK, (s - 1) // MOBA_BLOCK)
    hps = min(heads, MOBA_HEADS_PER_STEP)
    assert heads % hps == 0
    kern = functools.partial(_moba_prompt_kernel, n_blocks=n_blocks, n_sel=n_sel, scale=hd ** -0.5)
    return pl.pallas_call(
        kern,
        out_shape=jax.ShapeDtypeStruct((bsz, s, hd_all), F32),
        grid=(bsz, heads // hps, n_blocks),
        in_specs=[
            pl.BlockSpec((1, MOBA_BLOCK, hps * hd), lambda b, h, i: (b, i, h)),
            pl.BlockSpec((1, s, hps * hd), lambda b, h, i: (b, 0, h)),
            pl.BlockSpec((1, s, hps * hd), lambda b, h, i: (b, 0, h)),
        ],
        out_specs=pl.BlockSpec((1, MOBA_BLOCK, hps * hd), lambda b, h, i: (b, i, h)),
        scratch_shapes=[
            pltpu.VMEM((hps * n_blocks, MOBA_BLOCK, hd), BF16),
            pltpu.VMEM((hps * n_blocks, hd, MOBA_BLOCK), BF16),
            pltpu.VMEM((hps * nb_pad, hd), F32),
            pltpu.VMEM((hps * nb_pad, MOBA_BLOCK), F32),
            pltpu.VMEM((hps * n_blocks, MOBA_BLOCK, MOBA_BLOCK), F32),
        ],
        compiler_params=_params("parallel", "parallel", "arbitrary"),
        name="moba_prompt",
    )(q, k, v)


def _sample_select_kernel(q_ref, bm_ref, pt_ref, o_ref, *, heads, n_sel, ppb):
    rows, hd = q_ref.shape[1], q_ref.shape[2]
    nb = bm_ref.shape[2]
    n_pages = pt_ref.shape[2]
    q = q_ref[0]
    row_head = lax.broadcasted_iota(jnp.int32, (rows, nb), 0) % heads
    lane = lax.broadcasted_iota(jnp.int32, (rows, nb), 1)
    lane_pt = lax.broadcasted_iota(jnp.int32, (rows, n_pages), 1)
    lane_out = lax.broadcasted_iota(jnp.int32, (rows, LANES), 1)
    gate = jnp.zeros((rows, nb), F32)
    for h in range(heads):
        gate_h = _dot_nt(q, bm_ref[0, h], precision=HIGHEST)
        gate = jnp.where(row_head == h, gate_h, gate)
    pt = pt_ref[0].astype(F32)
    out = jnp.zeros((rows, LANES), F32)
    for s in range(n_sel):
        top = jnp.max(gate, axis=-1, keepdims=True)
        idx = jnp.min(jnp.where(gate == top, lane, nb), axis=-1, keepdims=True)
        gate = jnp.where(lane == idx, NEG_INF, gate)
        for p in range(ppb):
            phys = jnp.sum(jnp.where(lane_pt == idx * ppb + p, pt, 0.0), axis=-1, keepdims=True)
            out = out + jnp.where(lane_out == s * ppb + p, phys, 0.0)
    o_ref[0] = out.astype(jnp.int32)


def sample_select(q, bmean, page_table, *, heads, n_sel, ppb):
    dbsz, rows, hd = q.shape
    nb = bmean.shape[2]
    n_pages = page_table.shape[1]
    kern = functools.partial(_sample_select_kernel, heads=heads, n_sel=n_sel, ppb=ppb)
    return pl.pallas_call(
        kern,
        out_shape=jax.ShapeDtypeStruct((dbsz, rows, LANES), jnp.int32),
        grid=(dbsz,),
        in_specs=[
            pl.BlockSpec((1, rows, hd), lambda b: (b, 0, 0)),
            pl.BlockSpec((1, heads, nb, hd), lambda b: (b, 0, 0, 0)),
            pl.BlockSpec((1, 1, n_pages), lambda b: (b, 0, 0)),
        ],
        out_specs=pl.BlockSpec((1, rows, LANES), lambda b: (b, 0, 0)),
        compiler_params=_params("parallel"),
        name="sample_select",
    )(q, bmean, page_table.reshape(dbsz, 1, n_pages))


def _sample_attend_kernel(phys_ref, q_ref, kn_ref, vn_ref, ck_ref, cv_ref, o_ref, kbuf_ref, vbuf_ref, sem_ref,
                          *, n_slabs, t_len, heads, scale):
    n_gather = t_len * n_slabs
    step = pl.program_id(0)
    n_steps = pl.num_programs(0)
    slot = step % 2

    def slab_copies(of_step, to_slot):
        b, h = of_step // heads, of_step % heads
        copies = []
        for t in range(t_len):
            for s in range(n_slabs):
                g = t * n_slabs + s
                phys = phys_ref[((b * t_len + t) * heads + h) * n_slabs + s]
                copies.append(pltpu.make_async_copy(ck_ref.at[phys, :, h, :], kbuf_ref.at[to_slot, g],
                                                    sem_ref.at[to_slot, g]))
                copies.append(pltpu.make_async_copy(cv_ref.at[phys, :, h, :], vbuf_ref.at[to_slot, g],
                                                    sem_ref.at[to_slot, n_gather + g]))
        return copies

    @pl.when(step == 0)
    def _():
        for cp in slab_copies(step, slot):
            cp.start()

    @pl.when(step + 1 < n_steps)
    def _():
        for cp in slab_copies(step + 1, 1 - slot):
            cp.start()

    for cp in slab_copies(step, slot):
        cp.wait()
    k_refs = [kbuf_ref.at[slot, g] for g in range(n_gather)]
    v_refs = [vbuf_ref.at[slot, g] for g in range(n_gather)]

    q = q_ref[0, 0]
    rows = q.shape[0]
    q_bf = q.astype(BF16)
    k_new, v_new = kn_ref[0, 0], vn_ref[0, 0]
    row_c = lax.broadcasted_iota(jnp.int32, (rows, 1), 0)
    own = []
    for j in range(t_len):
        lg = jnp.sum(q * k_new[j:j + 1, :], axis=-1, keepdims=True) * scale
        own.append(jnp.where(row_c >= j, lg, NEG_INF))
    page = kbuf_ref.shape[2]
    row_p = lax.broadcasted_iota(jnp.int32, (rows, page), 0)
    past = []
    for s in range(n_slabs):
        lg = None
        for t in range(t_len):
            lg_t = _dot_nt(q_bf, k_refs[t * n_slabs + s][...].astype(BF16)) * scale
            lg = lg_t if lg is None else jnp.where(row_p == t, lg_t, lg)
        past.append(lg)
    top = own[0]
    for lg in own[1:]:
        top = jnp.maximum(top, lg)
    for lg in past:
        top = jnp.maximum(top, jnp.max(lg, axis=-1, keepdims=True))
    den = jnp.zeros((rows, 1), F32)
    out = jnp.zeros(q.shape, F32)
    for j in range(t_len):
        p = jnp.exp(own[j] - top)
        den = den + p
        out = out + p * v_new[j:j + 1, :]
    row_o = lax.broadcasted_iota(jnp.int32, q.shape, 0)
    for s in range(n_slabs):
        p = jnp.exp(past[s] - top)
        den = den + jnp.sum(p, axis=-1, keepdims=True)
        p_bf = p.astype(BF16)
        for t in range(t_len):
            o_t = _dot(p_bf, v_refs[t * n_slabs + s][...].astype(BF16))
            out = out + jnp.where(row_o == t, o_t, 0.0)
    o_ref[0, 0] = out / den


def sample_attend(q, k_new, v_new, cache_k, cache_v, phys, *, t_len, n_slabs):
    dbsz, heads, rows, hd = q.shape
    page = cache_k.shape[1]
    kern = functools.partial(_sample_attend_kernel, n_slabs=n_slabs, t_len=t_len, heads=heads, scale=hd ** -0.5)
    n_gather = t_len * n_slabs
    new_spec = pl.BlockSpec((1, 1, rows, hd), lambda i, ph: (i // heads, i % heads, 0, 0))
    any_spec = pl.BlockSpec(memory_space=pl.ANY)
    return pl.pallas_call(
        kern,
        out_shape=jax.ShapeDtypeStruct((dbsz, heads, rows, hd), F32),
        grid_spec=pltpu.PrefetchScalarGridSpec(
            num_scalar_prefetch=1,
            grid=(dbsz * heads,),
            in_specs=[new_spec, new_spec, new_spec, any_spec, any_spec],
            out_specs=new_spec,
            scratch_shapes=[
                pltpu.VMEM((2, n_gather, page, hd), F32),
                pltpu.VMEM((2, n_gather, page, hd), F32),
                pltpu.SemaphoreType.DMA((2, 2 * n_gather)),
            ],
        ),
        compiler_params=_params("arbitrary"),
        name="sample_attend",
    )(phys, q, k_new, v_new, cache_k, cache_v)


def _row_tile(n, target):
    tm = min(n, target)
    assert n % tm == 0
    return tm


def kernel(x_prompt, x_sample, state_gdn, state_conv, cache_k, cache_v, page_table, norm_mix, norm_ffn,
           gdn_w_in, gdn_conv_w, gdn_a_log, gdn_dt_bias, gdn_norm, gdn_w_out, kv_norm, w_kv, moba_w_q,
           moba_w_o, moe_w_group, moe_b_group, moe_w_router, moe_b_router, moe_w_gate, moe_w_up,
           moe_w_down, norm_final):
    bsz, s, d = x_prompt.shape
    dbsz, t, _ = x_sample.shape
    depth = norm_mix.shape[0]
    n_a = gdn_w_in.shape[0]
    conv_w, qkv_dim = gdn_conv_w.shape[1:]
    gdn_heads, dv = gdn_a_log.shape[1], gdn_norm.shape[1]
    z_dim = gdn_heads * dv
    n_pool, page, moba_heads, hd = cache_k.shape
    attn_dim = moba_heads * hd
    n_pages = page_table.shape[1]
    ppb = MOBA_BLOCK // page
    n_full = n_pages // ppb
    n_groups = moe_w_group.shape[-1]
    n_experts = moe_w_router.shape[-1]
    assert n_pages == n_full * ppb, "a partial past block (tail pages) is not supported"
    assert n_full >= MOBA_TOPK and t >= conv_w - 1 and s >= conv_w - 1 and t <= page

    groups = [(x_prompt.reshape(bsz * s, d), bsz, s), (x_sample.reshape(dbsz * t, d), dbsz, t)]
    tiles = [_row_tile(bsz * s, ROW_TILE), _row_tile(dbsz * t, ROW_TILE)]

    def moe(l, x, tm):
        w_route = jnp.zeros((d, LANES), F32).at[:, :n_groups].set(moe_w_group[l])
        w_route = w_route.at[:, n_groups:n_groups + n_experts].set(moe_w_router[l])
        b_route = jnp.zeros((1, LANES), F32).at[0, :n_groups].set(moe_b_group[l])
        b_route = b_route.at[0, n_groups:n_groups + n_experts].set(moe_b_router[l])
        return hier_moe_residual(x, norm_ffn[l], w_route, b_route, moe_w_gate[l], moe_w_up[l], moe_w_down[l],
                                 n_groups=n_groups, tm=tm, g_out=norm_final if l == depth - 1 else None)

    xs = [g[0] for g in groups]
    gdn_states, conv_states, kvs = [[], []], [[], []], [None, None]
    for l in range(depth):
        if l < n_a:
            in_dim = gdn_w_in.shape[2]
            in_pad = -(-in_dim // GDN_PROJ_COL_TILE) * GDN_PROJ_COL_TILE
            assert qkv_dim + z_dim + LANES <= in_pad and in_dim - qkv_dim - z_dim == 2 * gdn_heads
            w_in = jnp.pad(gdn_w_in[l], ((0, 0), (0, in_pad - in_dim))).astype(BF16)
            w_out = gdn_w_out[l].astype(BF16)
            for gi, (_, nb, tl) in enumerate(groups):
                tm = tiles[gi]
                proj = norm_matmul(xs[gi], norm_mix[l], w_in, tm=tm, tn=GDN_PROJ_COL_TILE).reshape(nb, tl, in_pad)
                tp = -(-tl // GDN_CHUNK) * GDN_CHUNK
                proj_p = jnp.pad(proj, ((0, 0), (0, tp - tl), (0, 0))) if tp != tl else proj
                if gi == 0:
                    conv0 = jnp.zeros((nb, conv_w - 1, qkv_dim), F32)
                    s0 = jnp.zeros((nb, gdn_heads, (qkv_dim // gdn_heads - dv) // 2, dv), F32)
                else:
                    conv0, s0 = state_conv[l], state_gdn[l]
                with_means = gi == 0 and l == 0
                og, s_fin, means = gdn_core(proj_p, conv0, s0, gdn_conv_w[l], gdn_a_log[l], gdn_dt_bias[l],
                                            gdn_norm[l], t_valid=tl, cache_k=cache_k if with_means else None,
                                            page_ids=page_table.reshape(-1) if with_means else None)
                if with_means:
                    bmean = means.reshape(dbsz, n_full, moba_heads, hd).transpose(0, 2, 1, 3)
                og = og[:, :tl].reshape(nb * tl, z_dim)
                gdn_states[gi].append(s_fin)
                conv_states[gi].append(proj[:, tl - (conv_w - 1):, :qkv_dim])
                xs[gi] = matmul_residual(og, w_out, xs[gi], tm=tm, tn=COL_TILE)
        else:
            j = l - n_a
            w_q = moba_w_q[j].astype(BF16)
            w_o = moba_w_o[j].astype(BF16)
            q = norm_matmul(xs[0], norm_mix[l], w_q, tm=tiles[0], tn=COL_TILE).reshape(bsz, s, attn_dim)
            k_p, v_p = kvs[0]
            o = moba_prompt(q, k_p.reshape(bsz, s, attn_dim), v_p.reshape(bsz, s, attn_dim), heads=moba_heads)
            xs[0] = matmul_residual(o.reshape(bsz * s, attn_dim), w_o, xs[0], tm=tiles[0], tn=COL_TILE)
            q = norm_matmul(xs[1], norm_mix[l], w_q, tm=tiles[1], tn=COL_TILE)
            k_s, v_s = kvs[1]
            n_sel = min(MOBA_TOPK, n_full)
            phys = sample_select(q.reshape(dbsz, t * moba_heads, hd), bmean, page_table,
                                 heads=moba_heads, n_sel=n_sel, ppb=ppb)
            phys = phys[:, :, :n_sel * ppb].reshape(-1)

            def head_major(a):
                a = a.reshape(dbsz, t, moba_heads, hd).transpose(0, 2, 1, 3)
                return jnp.pad(a, ((0, 0), (0, 0), (0, -t % 8), (0, 0)))

            o = sample_attend(head_major(q), head_major(k_s), head_major(v_s), cache_k, cache_v, phys,
                              t_len=t, n_slabs=n_sel * ppb)
            o = o[:, :, :t].transpose(0, 2, 1, 3).reshape(dbsz * t, attn_dim)
            xs[1] = matmul_residual(o, w_o, xs[1], tm=tiles[1], tn=COL_TILE)
        for gi in range(2):
            xs[gi] = moe(l, xs[gi], _row_tile(xs[gi].shape[0], MOE_TOKEN_TILE))
        if l == n_a - 1:
            w_kv_bf = w_kv.astype(BF16)
            for gi, (_, nb, tl) in enumerate(groups):
                kv = norm_matmul(xs[gi], kv_norm, w_kv_bf, tm=tiles[gi], tn=COL_TILE)
                kvs[gi] = (kv[:, :attn_dim].reshape(nb, tl, moba_heads, hd),
                           kv[:, attn_dim:].reshape(nb, tl, moba_heads, hd))
    y_p = xs[0].reshape(bsz, s, d)
    y_s = xs[1].reshape(dbsz, t, d)
    return (y_p, y_s, jnp.stack(gdn_states[0]), jnp.stack(conv_states[0]), kvs[0][0], kvs[0][1],
            jnp.stack(gdn_states[1]), jnp.stack(conv_states[1]), kvs[1][0], kvs[1][1])
```

```python
import functools

import jax
import jax.numpy as jnp
from jax import lax
from jax.experimental import pallas as pl
from jax.experimental.pallas import tpu as pltpu

RMS_EPS = 1e-6
L2_EPS = 1e-6
MOBA_BLOCK = 256
MOBA_TOPK = 3
MOBA_HEADS_PER_STEP = 4
TOPK_IN_GROUP = 2
GDN_CHUNK = 64
LANES = 128
SUBLANES = 8
ROW_TILE = 1024
COL_TILE = 1024
GDN_PROJ_COL_TILE = 11 * LANES
MOE_TOKEN_TILE = 2048
MOE_ROW_BLOCK = 256
MOE_EXPERTS_PER_STEP = 2
MOE_WEIGHT_SLOTS = 4
CONV_PAD_ROWS = SUBLANES
VMEM_LIMIT = 56 * 1024 * 1024

F32 = jnp.float32
BF16 = jnp.bfloat16
HIGHEST = lax.Precision.HIGHEST
NEG_INF = float("-inf")


def _params(*sem):
    return pltpu.CompilerParams(dimension_semantics=sem, vmem_limit_bytes=VMEM_LIMIT)


def _rms(x, g):
    return x * lax.rsqrt(jnp.mean(x * x, axis=-1, keepdims=True) + RMS_EPS) * g


def _silu(x):
    return x * jax.nn.sigmoid(x)


def _softplus(x):
    return jnp.maximum(x, 0.0) + jnp.log1p(jnp.exp(-jnp.abs(x)))


def _dot(a, b):
    return jnp.dot(a, b, preferred_element_type=F32)


def _dot_nt(a, b, precision=None):
    return lax.dot_general(a, b, (((1,), (1,)), ((), ())), precision=precision, preferred_element_type=F32)


def _dot_tn(a, b):
    return lax.dot_general(a, b, (((0,), (0,)), ((), ())), preferred_element_type=F32)


def _norm_matmul_kernel(x_ref, g_ref, w_ref, o_ref, xn_ref):
    @pl.when(pl.program_id(1) == 0)
    def _():
        xn_ref[...] = _rms(x_ref[...], g_ref[...]).astype(BF16)

    o_ref[...] = _dot(xn_ref[...], w_ref[...])


def norm_matmul(x, g, w, *, tm, tn):
    n, d = x.shape
    m = w.shape[1]
    assert n % tm == 0 and m % tn == 0
    return pl.pallas_call(
        _norm_matmul_kernel,
        out_shape=jax.ShapeDtypeStruct((n, m), F32),
        grid=(n // tm, m // tn),
        in_specs=[
            pl.BlockSpec((tm, d), lambda i, j: (i, 0)),
            pl.BlockSpec((1, d), lambda i, j: (0, 0)),
            pl.BlockSpec((d, tn), lambda i, j: (0, j)),
        ],
        out_specs=pl.BlockSpec((tm, tn), lambda i, j: (i, j)),
        scratch_shapes=[pltpu.VMEM((tm, d), BF16)],
        compiler_params=_params("parallel", "arbitrary"),
        name="norm_matmul",
    )(x, g.reshape(1, d), w)


def _matmul_residual_kernel(a_ref, w_ref, r_ref, o_ref):
    o_ref[...] = r_ref[...] + _dot(a_ref[...].astype(BF16), w_ref[...])


def matmul_residual(a, w, resid, *, tm, tn):
    n, k = a.shape
    m = w.shape[1]
    assert n % tm == 0 and m % tn == 0
    return pl.pallas_call(
        _matmul_residual_kernel,
        out_shape=jax.ShapeDtypeStruct((n, m), F32),
        grid=(n // tm, m // tn),
        in_specs=[
            pl.BlockSpec((tm, k), lambda i, j: (i, 0)),
            pl.BlockSpec((k, tn), lambda i, j: (0, j)),
            pl.BlockSpec((tm, tn), lambda i, j: (i, j)),
        ],
        out_specs=pl.BlockSpec((tm, tn), lambda i, j: (i, j)),
        compiler_params=_params("parallel", "parallel"),
        name="matmul_residual",
    )(a, w, resid)


def _gdn_kernel(page_ids_ref, qkv_ref, z_ref, ba_ref, cprev_ref, s0_ref, cw_ref, par_ref, ng_ref, *refs,
                chunk, heads, dk, dv, t_valid, n_pages_step, ppb):
    del page_ids_ref
    page_refs = refs[:n_pages_step]
    if n_pages_step:
        og_ref, s_ref, bm_ref, xbuf_ref = refs[n_pages_step:]
        for n in range(n_pages_step // ppb):
            acc = None
            for p in range(ppb):
                part = jnp.sum(page_refs[n * ppb + p][0], axis=0)
                acc = part if acc is None else acc + part
            bm_ref[0, n] = acc / MOBA_BLOCK
    else:
        og_ref, s_ref, xbuf_ref = refs
    c = pl.program_id(1)
    hk = heads * dk
    hist = CONV_PAD_ROWS
    conv_w = cw_ref.shape[0]

    @pl.when(c == 0)
    def _():
        xbuf_ref[0:hist, :] = cprev_ref[0]
        s_ref[...] = s0_ref[...]

    xbuf_ref[hist:hist + chunk, :] = qkv_ref[0]

    row = lax.broadcasted_iota(jnp.int32, (chunk, LANES), 0)
    valid = (row + c * chunk) < t_valid
    ba = ba_ref[0]
    neg_a = -jnp.exp(par_ref[0:1, :])
    dt_bias = par_ref[1:2, :]
    beta_all = jnp.where(valid, jax.nn.sigmoid(ba), 0.0)
    g_all = jnp.where(valid, neg_a * _softplus(ba + dt_bias), 0.0)

    r_i = lax.broadcasted_iota(jnp.int32, (chunk, chunk), 0)
    c_i = lax.broadcasted_iota(jnp.int32, (chunk, chunk), 1)
    incl = c_i <= r_i
    strict = c_i < r_i
    gcum_all = jnp.dot(incl.astype(F32), g_all, precision=HIGHEST, preferred_element_type=F32)

    def conv_silu(col, width):
        acc = None
        for i in range(conv_w):
            lo = hist - (conv_w - 1) + i
            term = xbuf_ref[lo:lo + chunk, col:col + width] * cw_ref[i:i + 1, col:col + width]
            acc = term if acc is None else acc + term
        return _silu(acc)

    def l2n(x):
        return x * lax.rsqrt(jnp.sum(x * x, axis=-1, keepdims=True) + L2_EPS)

    gpad =jnp.concatenate([gcum_all, jnp.zeros((LANES - chunk, LANES), F32)], axis=0) if chunk < LANES else gcum_all
    gcum_t = gpad.T

    hs = range(heads)
    q = [l2n(conv_silu(h * dk, dk)) * (dk ** -0.5) for h in hs]
    k = [l2n(conv_silu(hk + h * dk, dk)) for h in hs]
    v = [conv_silu(2 * hk + h * dv, dv) for h in hs]
    xbuf_ref[0:hist, :] = xbuf_ref[chunk:chunk + hist, :]
    q_bf = [x.astype(BF16) for x in q]
    k_bf = [x.astype(BF16) for x in k]
    beta = [beta_all[:, h:h + 1] for h in hs]
    gc = [gcum_all[:, heads + h:heads + h + 1] for h in hs]
    decay = [jnp.exp(jnp.where(incl, gc[h] - gcum_t[heads + h:heads + h + 1, :chunk], NEG_INF)) for h in hs]
    egc = [jnp.exp(x) for x in gc]
    s = [s_ref[0, h] for h in hs]
    s_bf = [x.astype(BF16) for x in s]
    kk = [_dot_nt(k_bf[h], k_bf[h]) for h in hs]
    k_s = [_dot(k_bf[h], s_bf[h]) for h in hs]
    q_s = [_dot(q_bf[h], s_bf[h]) for h in hs]
    qk = [(_dot_nt(q_bf[h], k_bf[h]) * decay[h]).astype(BF16) for h in hs]
    power = [jnp.where(strict, beta[h] * decay[h] * kk[h], 0.0).astype(BF16) for h in hs]
    u = [beta[h] * (v[h] - egc[h] * k_s[h]) for h in hs]
    powers = [power]
    span = 2
    while span < chunk:
        powers.append([_dot(p, p).astype(BF16) for p in powers[-1]])
        span *= 2
    for power in reversed(powers[1:]):
        u = [u[h] + _dot(power[h], u[h].astype(BF16)) for h in hs]
    u = [u[h] - _dot(powers[0][h], u[h].astype(BF16)) for h in hs]
    u_bf = [x.astype(BF16) for x in u]
    o = [egc[h] * q_s[h] + _dot(qk[h], u_bf[h]) for h in hs]
    glast = [x[chunk - 1:chunk, :] for x in gc]
    kd = [(k[h] * jnp.exp(glast[h] - gc[h])).astype(BF16) for h in hs]
    for h in hs:
        s_ref[0, h] = jnp.exp(glast[h]) * s[h] + _dot_tn(kd[h], u_bf[h])
    for h in hs:
        zg = _silu(z_ref[0, :, h * dv:(h + 1) * dv])
        on = o[h] * lax.rsqrt(jnp.mean(o[h] * o[h], axis=-1, keepdims=True) + RMS_EPS)
        og_ref[0, :, h * dv:(h + 1) * dv] = on * ng_ref[...] * zg


def gdn_core(proj, conv_prev, s0, conv_w, a_log, dt_bias, norm_g, *, t_valid, cache_k=None, page_ids=None):
    bsz, tp, _ = proj.shape
    heads = a_log.shape[0]
    dv = norm_g.shape[0]
    qkv_dim = conv_w.shape[1]
    dk = (qkv_dim // heads - dv) // 2
    z_dim = heads * dv
    chunk = GDN_CHUNK
    assert tp % chunk == 0 and dk == LANES and dv == LANES and 2 * heads <= LANES
    assert conv_w.shape[0] - 1 <= CONV_PAD_ROWS <= chunk
    par = jnp.zeros((8, LANES), F32)
    par = par.at[0, heads:2 * heads].set(a_log).at[1, heads:2 * heads].set(dt_bias)
    cprev = jnp.pad(conv_prev, ((0, 0), (CONV_PAD_ROWS - conv_prev.shape[1], 0), (0, 0)))
    nq = qkv_dim // LANES
    n_chunks = tp // chunk
    out_shape = [jax.ShapeDtypeStruct((bsz, tp, z_dim), F32), jax.ShapeDtypeStruct((bsz, heads, dk, dv), F32)]
    out_specs = [pl.BlockSpec((1, chunk, z_dim), lambda b, c, ids: (b, c, 0)),
                 pl.BlockSpec((1, heads, dk, dv), lambda b, c, ids: (b, 0, 0, 0))]
    page_specs, page_args, n_pages_step, ppb = [], [], 0, 1
    if cache_k is None:
        page_ids = jnp.zeros((1,), jnp.int32)
    else:
        _, page, kv_heads, hd = cache_k.shape
        ppb = MOBA_BLOCK // page
        n_pages_step = page_ids.shape[0] // (bsz * n_chunks)
        assert n_pages_step * bsz * n_chunks == page_ids.shape[0] and n_pages_step % ppb == 0

        def page_spec(i):
            return pl.BlockSpec((1, page, kv_heads, hd),
                                lambda b, c, ids: (ids[(b * n_chunks + c) * n_pages_step + i], 0, 0, 0))

        page_specs = [page_spec(i) for i in range(n_pages_step)]
        page_args = [cache_k] * n_pages_step
        out_shape.append(jax.ShapeDtypeStruct((bsz * n_chunks, n_pages_step // ppb, kv_heads, hd), F32))
        out_specs.append(pl.BlockSpec((1, n_pages_step // ppb, kv_heads, hd),
                                      lambda b, c, ids: (b * n_chunks + c, 0, 0, 0)))
    kern = functools.partial(_gdn_kernel, chunk=chunk, heads=heads, dk=dk, dv=dv, t_valid=t_valid,
                             n_pages_step=n_pages_step, ppb=ppb)
    outs = pl.pallas_call(
        kern,
        out_shape=tuple(out_shape),
        grid_spec=pltpu.PrefetchScalarGridSpec(
            num_scalar_prefetch=1,
            grid=(bsz, n_chunks),
            in_specs=[
                pl.BlockSpec((1, chunk, qkv_dim), lambda b, c, ids: (b, c, 0)),
                pl.BlockSpec((1, chunk, z_dim), lambda b, c, ids: (b, c, qkv_dim // z_dim)),
                pl.BlockSpec((1, chunk, LANES), lambda b, c, ids: (b, c, nq + z_dim // LANES)),
                pl.BlockSpec((1, CONV_PAD_ROWS, qkv_dim), lambda b, c, ids: (b, 0, 0)),
                pl.BlockSpec((1, heads, dk, dv), lambda b, c, ids: (b, 0, 0, 0)),
                pl.BlockSpec(conv_w.shape, lambda b, c, ids: (0, 0)),
                pl.BlockSpec((8, LANES), lambda b, c, ids: (0, 0)),
                pl.BlockSpec((1, dv), lambda b, c, ids: (0, 0)),
            ] + page_specs,
            out_specs=tuple(out_specs),
            scratch_shapes=[pltpu.VMEM((CONV_PAD_ROWS + chunk, qkv_dim), F32)],
        ),
        compiler_params=_params("parallel", "arbitrary"),
        name="gdn_core",
    )(page_ids, proj, proj, proj, cprev, s0, conv_w, par, norm_g.reshape(1, dv), *page_args)
    if cache_k is None:
        return outs[0], outs[1], None
    return outs[0], outs[1], outs[2].reshape(-1, kv_heads, hd)


def _moe_route_kernel(x_ref, g_ref, wr_ref, br_ref, pos_ref, wts_ref, seg_ref, *, n_groups, n_experts):
    tm = x_ref.shape[0]
    epg = n_experts // n_groups
    xn = _rms(x_ref[...], g_ref[...])
    logits = jnp.dot(xn, wr_ref[...], precision=HIGHEST, preferred_element_type=F32) + br_ref[...]
    lt = logits.T
    sub = lax.broadcasted_iota(jnp.int32, lt.shape, 0)
    is_g = sub < n_groups
    gl = jnp.where(is_g, lt, NEG_INF)
    ge = jnp.exp(gl - jnp.max(gl, axis=0, keepdims=True))
    pg = ge / jnp.sum(ge, axis=0, keepdims=True)
    pg_sel = jnp.max(pg, axis=0, keepdims=True)
    g_idx = jnp.min(jnp.where(is_g & (pg == pg_sel), sub, LANES), axis=0, keepdims=True)
    lo = n_groups + g_idx * epg
    in_g = (sub >= lo) & (sub < lo + epg)
    el = jnp.where(in_g, lt, NEG_INF)
    ee = jnp.exp(el - jnp.max(el, axis=0, keepdims=True))
    pe = ee / jnp.sum(ee, axis=0, keepdims=True)
    rest = jnp.where(in_g, pe, NEG_INF)
    tops = []
    for _ in range(TOPK_IN_GROUP):
        top_v = jnp.max(rest, axis=0, keepdims=True)
        top_i = jnp.min(jnp.where(rest == top_v, sub, LANES), axis=0, keepdims=True)
        tops.append((top_v, top_i))
        rest = jnp.where(sub == top_i, NEG_INF, rest)
    denom = tops[0][0]
    for top_v, _ in tops[1:]:
        denom = denom + top_v
    wts_ref[0] = jnp.concatenate([pg_sel * top_v / denom for top_v, _ in tops], axis=1)

    onehot = [jnp.where(sub == top_i, 1.0, 0.0) for _, top_i in tops]
    span = min(tm, 512)
    earlier = jnp.where(lax.broadcasted_iota(jnp.int32, (span, span), 0)
                        < lax.broadcasted_iota(jnp.int32, (span, span), 1), 1.0, 0.0).astype(BF16)
    count = [jnp.sum(oh, axis=1, keepdims=True) for oh in onehot]
    total = count[0]
    for c in count[1:]:
        total = total + c
    padded = jnp.floor((total + (SUBLANES - 1)) * (1.0 / SUBLANES)) * SUBLANES
    below = jnp.where(lax.broadcasted_iota(jnp.int32, (LANES, LANES), 1) < lax.broadcasted_iota(jnp.int32, (LANES, LANES), 0),
                      1.0, 0.0)
    start = jnp.dot(below, jnp.broadcast_to(padded, (LANES, LANES)), precision=HIGHEST,
                    preferred_element_type=F32)[:, 0:1]
    pos = []
    base = start
    for oh, c in zip(onehot, count):
        for lo_n in range(0, tm, span):
            oh_n = oh[:, lo_n:lo_n + span]
            before = _dot(oh_n.astype(BF16), earlier)
            pos.append(jnp.sum(oh_n * (base + before), axis=0, keepdims=True))
            base = base + jnp.sum(oh_n, axis=1, keepdims=True)
    pos_ref[0] = jnp.concatenate(pos, axis=1).astype(jnp.int32)
    lane = lax.broadcasted_iota(jnp.int32, (LANES, LANES), 1)
    cols = jnp.where(lane == 0, start, 0.0) + jnp.where(lane == 1, total, 0.0)
    seg_ref[0] = cols.T[0:8, :].astype(jnp.int32)


def _moe_expert_kernel(pos_ref, wts_ref, seg_ref, x_ref, g_ref, g_out_ref, wg_hbm, wu_hbm, wd_hbm, o_ref,
                       rows_ref, stage_ref, wg_ref, wu_ref, wd_ref, wsem_ref,
                       *, n_groups, n_experts, n_slots, row_block, norm_out):
    step = pl.program_id(1)
    tm, d = x_ref.shape
    n_buf, experts_step = wg_ref.shape[:2]
    chunk = stage_ref.shape[0]
    tile = stage_ref.shape[1:]

    steps_tile = pl.num_programs(1)
    n_steps = pl.num_programs(0) * steps_tile
    g_step = pl.program_id(0) * steps_tile + step
    ahead = n_buf - 1

    def weight_copies(of_step):
        slot = of_step % n_buf
        first = (of_step % steps_tile) * experts_step
        return [pltpu.make_async_copy(hbm.at[pl.ds(first, experts_step)], buf.at[slot], wsem_ref.at[slot, i])
                for i, (hbm, buf) in enumerate(((wg_hbm, wg_ref), (wu_hbm, wu_ref), (wd_hbm, wd_ref)))]

    @pl.when(g_step == 0)
    def _():
        for s in range(ahead):
            @pl.when(s < n_steps)
            def _(s=s):
                for cp in weight_copies(s):
                    cp.start()

    @pl.when(g_step + ahead < n_steps)
    def _():
        for cp in weight_copies(g_step + ahead):
            cp.start()

    for cp in weight_copies(g_step):
        cp.wait()
    w_slot = g_step % n_buf

    @pl.when(step == 0)
    def _():
        end = 0
        for j in range(n_experts):
            end = seg_ref[0, 0, n_groups + j] + seg_ref[0, 1, n_groups + j]
            last_rows = jnp.maximum((end - 1) // SUBLANES * SUBLANES, 0)
            rows_ref[pl.ds(last_rows, SUBLANES)] = jnp.zeros((SUBLANES,) + tile, F32)
        tail = (end + SUBLANES - 1) // SUBLANES * SUBLANES
        rows_ref[pl.ds(tail, row_block)] = jnp.zeros((row_block,) + tile, F32)

        for c in range(tm // chunk):
            stage_ref[...] = _rms(x_ref[c * chunk:(c + 1) * chunk, :], g_ref[...]).reshape((chunk,) + tile)

            def scatter(n, carry, c=c):
                row = stage_ref[n]
                for k in range(n_slots):
                    rows_ref[pos_ref[0, 0, k * tm + c * chunk + n]] = row
                return carry

            lax.fori_loop(0, chunk, scatter, 0, unroll=8)

    for j in range(experts_step):
        lane = n_groups + step * experts_step + j
        start = seg_ref[0, 0, lane]
        count = seg_ref[0, 1, lane]

        def run_block(i, carry, j=j, start=start, count=count):
            r0 = start + i * row_block
            xin = rows_ref[pl.ds(r0, row_block)]
            xb = xin.reshape(row_block, d).astype(BF16)
            he = _silu(_dot(xb, wg_ref[w_slot, j])) * _dot(xb, wu_ref[w_slot, j])
            y = _dot(he.astype(BF16), wd_ref[w_slot, j]).reshape(xin.shape)
            mine = lax.broadcasted_iota(jnp.int32, xin.shape, 0) < count - i * row_block
            rows_ref[pl.ds(r0, row_block)] = jnp.where(mine, y, xin)
            return carry

        lax.fori_loop(0, (count + row_block - 1) // row_block, run_block, 0)

    @pl.when(step == pl.num_programs(1) - 1)
    def _():
        for c in range(tm // chunk):
            def gather(n, carry, c=c):
                acc = None
                for k in range(n_slots):
                    i = k * tm + c * chunk + n
                    term = wts_ref[0, 0, i] * rows_ref[pos_ref[0, 0, i]]
                    acc = term if acc is None else acc + term
                stage_ref[n] = acc
                return carry

            lax.fori_loop(0, chunk, gather, 0, unroll=8)
            out = x_ref[c * chunk:(c + 1) * chunk, :] + stage_ref[...].reshape(chunk, d)
            o_ref[c * chunk:(c + 1) * chunk, :] = _rms(out, g_out_ref[...]) if norm_out else out


def hier_moe_residual(x, g, w_route, b_route, w_gate, w_up, w_down, *, n_groups, tm, g_out=None):
    n, d = x.shape
    n_experts, _, ff = w_gate.shape
    n_slots = TOPK_IN_GROUP
    assert n % tm == 0 and tm % LANES == 0 and n_groups + n_experts <= LANES
    n_tiles = n // tm
    row_block = min(MOE_ROW_BLOCK, tm)
    rows_cap = -(-(n_slots * tm + n_experts * (SUBLANES - 1)) // SUBLANES) * SUBLANES + row_block
    route =functools.partial(_moe_route_kernel, n_groups=n_groups, n_experts=n_experts)
    pos, wts, seg = pl.pallas_call(
        route,
        out_shape=(jax.ShapeDtypeStruct((n_tiles, 1, n_slots * tm), jnp.int32),
                   jax.ShapeDtypeStruct((n_tiles, 1, n_slots * tm), F32),
                   jax.ShapeDtypeStruct((n_tiles, 8, LANES), jnp.int32)),
        grid=(n_tiles,),
        in_specs=[
            pl.BlockSpec((tm, d), lambda i: (i, 0)),
            pl.BlockSpec((1, d), lambda i: (0, 0)),
            pl.BlockSpec((d, LANES), lambda i: (0, 0)),
            pl.BlockSpec((1, LANES), lambda i: (0, 0)),
        ],
        out_specs=(pl.BlockSpec((1, 1, n_slots * tm), lambda i: (i, 0, 0)),
                   pl.BlockSpec((1, 1, n_slots * tm), lambda i: (i, 0, 0)),
                   pl.BlockSpec((1, 8, LANES), lambda i: (i, 0, 0))),
        compiler_params=_params("parallel"),
        name="moe_route",
    )(x, g.reshape(1, d), w_route, b_route)

    def smem_spec(shape):
        return pl.BlockSpec((1,) + shape, lambda i, e: (i, 0, 0), memory_space=pltpu.SMEM)

    expert = functools.partial(_moe_expert_kernel, n_groups=n_groups, n_experts=n_experts, n_slots=n_slots,
                               row_block=row_block, norm_out=g_out is not None)
    once = pl.Buffered(1)
    eps = MOE_EXPERTS_PER_STEP
    assert n_experts % eps == 0
    return pl.pallas_call(
        expert,
        out_shape=jax.ShapeDtypeStruct((n, d), F32),
        grid=(n_tiles, n_experts // eps),
        in_specs=[
            smem_spec((1, n_slots * tm)),
            smem_spec((1, n_slots * tm)),
            smem_spec((8, LANES)),
            pl.BlockSpec((tm, d), lambda i, e: (i, 0), pipeline_mode=once),
            pl.BlockSpec((1, d), lambda i, e: (0, 0)),
            pl.BlockSpec((1, d), lambda i, e: (0, 0)),
            pl.BlockSpec(memory_space=pl.ANY),
            pl.BlockSpec(memory_space=pl.ANY),
            pl.BlockSpec(memory_space=pl.ANY),
        ],
        out_specs=pl.BlockSpec((tm, d), lambda i, e: (i, 0), pipeline_mode=once),
        scratch_shapes=[pltpu.VMEM((rows_cap, d // LANES, LANES), F32),
                        pltpu.VMEM((min(tm, MOE_ROW_BLOCK), d // LANES, LANES), F32),
                        pltpu.VMEM((MOE_WEIGHT_SLOTS, eps, d, ff), BF16),
                        pltpu.VMEM((MOE_WEIGHT_SLOTS, eps, d, ff), BF16),
                        pltpu.VMEM((MOE_WEIGHT_SLOTS, eps, ff, d), BF16),
                        pltpu.SemaphoreType.DMA((MOE_WEIGHT_SLOTS, 3))],
        compiler_params=_params("arbitrary", "arbitrary"),
        name="moe_experts",
    )(pos, wts, seg, x, g.reshape(1, d), (g if g_out is None else g_out).reshape(1, d), w_gate, w_up, w_down)


def _moba_prompt_kernel(q_ref, k_ref, v_ref, o_ref, kbf_ref, vt_ref, kmean_ref, sel_ref, lg_ref,
                        *, n_blocks, n_sel, scale):
    qi = pl.program_id(2)
    blk = MOBA_BLOCK
    hd = kbf_ref.shape[2]
    hs = range(q_ref.shape[2] // hd)
    nb_pad = kmean_ref.shape[0] // len(hs)

    @pl.when(qi == 0)
    def _():
        kmean_ref[...] = jnp.zeros_like(kmean_ref)
        for h in hs:
            for j in range(n_blocks):
                kj = k_ref[0, j * blk:(j + 1) * blk, h * hd:(h + 1) * hd]
                kmean_ref[h * nb_pad + j:h * nb_pad + j + 1, :] = jnp.mean(kj, axis=0, keepdims=True)
                kbf_ref[h * n_blocks + j] = kj.astype(BF16)
                vt_ref[h * n_blocks + j] = v_ref[0, j * blk:(j + 1) * blk, h * hd:(h + 1) * hd].T.astype(BF16)

    q = [q_ref[0, :, h * hd:(h + 1) * hd] for h in hs]
    q_bf = [x.astype(BF16) for x in q]
    gate = [_dot_nt(kmean_ref[h * nb_pad:(h + 1) * nb_pad, :], q[h], precision=HIGHEST) for h in hs]
    blk_i = lax.broadcasted_iota(jnp.int32, gate[0].shape, 0)
    past = blk_i < qi
    for h in hs:
        for j in range(n_blocks):
            gj = gate[h][j:j + 1, :]
            ahead = ((gate[h] > gj) | ((gate[h] == gj) & (blk_i < j))) & past
            rank = jnp.sum(jnp.where(ahead, 1.0, 0.0), axis=0, keepdims=True)
            sel_ref[h * nb_pad + j:h * nb_pad + j + 1, :] = jnp.where(rank < n_sel, 1.0, 0.0)

    def past_block(j, tops):
        new = []
        for h in hs:
            sc = _dot_nt(kbf_ref[h * n_blocks + j], q_bf[h]) * scale
            sc = jnp.where(sel_ref[pl.ds(h * nb_pad + j, 1), :] > 0.5, sc, NEG_INF)
            lg_ref[h * n_blocks + j] = sc
            new.append(jnp.maximum(tops[h], jnp.max(sc, axis=0, keepdims=True)))
        return tuple(new)

    tops = lax.fori_loop(0, qi, past_block, tuple(jnp.full((1, blk), NEG_INF, F32) for _ in hs))
    key_i = lax.broadcasted_iota(jnp.int32, (blk, blk), 0)
    qry_i = lax.broadcasted_iota(jnp.int32, (blk, blk), 1)
    top = []
    for h in hs:
        own = jnp.where(key_i <= qry_i, _dot_nt(kbf_ref[h * n_blocks + qi], q_bf[h]) * scale, NEG_INF)
        lg_ref[h * n_blocks + qi] = own
        top.append(jnp.maximum(tops[h], jnp.max(own, axis=0, keepdims=True)))

    def weigh_block(j, carry):
        new = []
        for h in hs:
            den, out = carry[h]
            p = jnp.exp(lg_ref[h * n_blocks + j] - top[h])
            new.append((den + jnp.sum(p, axis=0, keepdims=True), out + _dot(vt_ref[h * n_blocks + j], p.astype(BF16))))
        return tuple(new)

    acc = lax.fori_loop(0, qi + 1, weigh_block,
                        tuple((jnp.zeros((1, blk), F32), jnp.zeros((hd, blk), F32)) for _ in hs))
    for h in hs:
        den, out = acc[h]
        o_ref[0, :, h * hd:(h + 1) * hd] = (out / den).T


def moba_prompt(q, k, v, *, heads):
    bsz, s, hd_all = q.shape
    hd = hd_all // heads
    assert s % MOBA_BLOCK == 0 and hd == LANES
    n_blocks = s // MOBA_BLOCK
    nb_pad = -(-n_blocks // SUBLANES) * SUBLANES
    n_sel = min(MOBA_TOPK, (s - 1) // MOBA_BLOCK)
    hps = min(heads, MOBA_HEADS_PER_STEP)
    assert heads % hps == 0
    kern = functools.partial(_moba_prompt_kernel, n_blocks=n_blocks, n_sel=n_sel, scale=hd ** -0.5)
    return pl.pallas_call(
        kern,
        out_shape=jax.ShapeDtypeStruct((bsz, s, hd_all), F32),
        grid=(bsz, heads // hps, n_blocks),
        in_specs=[
            pl.BlockSpec((1, MOBA_BLOCK, hps * hd), lambda b, h, i: (b, i, h)),
            pl.BlockSpec((1, s, hps * hd), lambda b, h, i: (b, 0, h)),
            pl.BlockSpec((1, s, hps * hd), lambda b, h, i: (b, 0, h)),
        ],
        out_specs=pl.BlockSpec((1, MOBA_BLOCK, hps * hd), lambda b, h, i: (b, i, h)),
        scratch_shapes=[
            pltpu.VMEM((hps * n_blocks, MOBA_BLOCK, hd), BF16),
            pltpu.VMEM((hps * n_blocks, hd, MOBA_BLOCK), BF16),
            pltpu.VMEM((hps * nb_pad, hd), F32),
            pltpu.VMEM((hps * nb_pad, MOBA_BLOCK), F32),
            pltpu.VMEM((hps * n_blocks, MOBA_BLOCK, MOBA_BLOCK), F32),
        ],
        compiler_params=_params("parallel", "parallel", "arbitrary"),
        name="moba_prompt",
    )(q, k, v)


def _sample_select_kernel(q_ref, bm_ref, pt_ref, o_ref, *, heads, n_sel, ppb):
    rows, hd = q_ref.shape[1], q_ref.shape[2]
    nb = bm_ref.shape[2]
    n_pages = pt_ref.shape[2]
    q = q_ref[0]
    row_head = lax.broadcasted_iota(jnp.int32, (rows, nb), 0) % heads
    lane = lax.broadcasted_iota(jnp.int32, (rows, nb), 1)
    lane_pt = lax.broadcasted_iota(jnp.int32, (rows, n_pages), 1)
    lane_out = lax.broadcasted_iota(jnp.int32, (rows, LANES), 1)
    gate = jnp.zeros((rows, nb), F32)
    for h in range(heads):
        gate_h = _dot_nt(q, bm_ref[0, h], precision=HIGHEST)
        gate = jnp.where(row_head == h, gate_h, gate)
    pt = pt_ref[0].astype(F32)
    out = jnp.zeros((rows, LANES), F32)
    for s in range(n_sel):
        top = jnp.max(gate, axis=-1, keepdims=True)
        idx = jnp.min(jnp.where(gate == top, lane, nb), axis=-1, keepdims=True)
        gate = jnp.where(lane == idx, NEG_INF, gate)
        for p in range(ppb):
            phys = jnp.sum(jnp.where(lane_pt == idx * ppb + p, pt, 0.0), axis=-1, keepdims=True)
            out = out + jnp.where(lane_out == s * ppb + p, phys, 0.0)
    o_ref[0] = out.astype(jnp.int32)


def sample_select(q, bmean, page_table, *, heads, n_sel, ppb):
    dbsz, rows, hd = q.shape
    nb = bmean.shape[2]
    n_pages = page_table.shape[1]
    kern = functools.partial(_sample_select_kernel, heads=heads, n_sel=n_sel, ppb=ppb)
    return pl.pallas_call(
        kern,
        out_shape=jax.ShapeDtypeStruct((dbsz, rows, LANES), jnp.int32),
        grid=(dbsz,),
        in_specs=[
            pl.BlockSpec((1, rows, hd), lambda b: (b, 0, 0)),
            pl.BlockSpec((1, heads, nb, hd), lambda b: (b, 0, 0, 0)),
            pl.BlockSpec((1, 1, n_pages), lambda b: (b, 0, 0)),
        ],
        out_specs=pl.BlockSpec((1, rows, LANES), lambda b: (b, 0, 0)),
        compiler_params=_params("parallel"),
        name="sample_select",
    )(q, bmean, page_table.reshape(dbsz, 1, n_pages))


def _sample_attend_kernel(phys_ref, q_ref, kn_ref, vn_ref, ck_ref, cv_ref, o_ref, kbuf_ref, vbuf_ref, sem_ref,
                          *, n_slabs, t_len, heads, scale):
    n_gather = t_len * n_slabs
    step = pl.program_id(0)
    n_steps = pl.num_programs(0)
    slot = step % 2

    def slab_copies(of_step, to_slot):
        b, h = of_step // heads, of_step % heads
        copies = []
        for t in range(t_len):
            for s in range(n_slabs):
                g = t * n_slabs + s
                phys = phys_ref[((b * t_len + t) * heads + h) * n_slabs + s]
                copies.append(pltpu.make_async_copy(ck_ref.at[phys, :, h, :], kbuf_ref.at[to_slot, g],
                                                    sem_ref.at[to_slot, g]))
                copies.append(pltpu.make_async_copy(cv_ref.at[phys, :, h, :], vbuf_ref.at[to_slot, g],
                                                    sem_ref.at[to_slot, n_gather + g]))
        return copies

    @pl.when(step == 0)
    def _():
        for cp in slab_copies(step, slot):
            cp.start()

    @pl.when(step + 1 < n_steps)
    def _():
        for cp in slab_copies(step + 1, 1 - slot):
            cp.start()

    for cp in slab_copies(step, slot):
        cp.wait()
    k_refs = [kbuf_ref.at[slot, g] for g in range(n_gather)]
    v_refs = [vbuf_ref.at[slot, g] for g in range(n_gather)]

    q = q_ref[0, 0]
    rows = q.shape[0]
    q_bf = q.astype(BF16)
    k_new, v_new = kn_ref[0, 0], vn_ref[0, 0]
    row_c = lax.broadcasted_iota(jnp.int32, (rows, 1), 0)
    own = []
    for j in range(t_len):
        lg = jnp.sum(q * k_new[j:j + 1, :], axis=-1, keepdims=True) * scale
        own.append(jnp.where(row_c >= j, lg, NEG_INF))
    page = kbuf_ref.shape[2]
    row_p = lax.broadcasted_iota(jnp.int32, (rows, page), 0)
    past = []
    for s in range(n_slabs):
        lg = None
        for t in range(t_len):
            lg_t = _dot_nt(q_bf, k_refs[t * n_slabs + s][...].astype(BF16)) * scale
            lg = lg_t if lg is None else jnp.where(row_p == t, lg_t, lg)
        past.append(lg)
    top = own[0]
    for lg in own[1:]:
        top = jnp.maximum(top, lg)
    for lg in past:
        top = jnp.maximum(top, jnp.max(lg, axis=-1, keepdims=True))
    den = jnp.zeros((rows, 1), F32)
    out = jnp.zeros(q.shape, F32)
    for j in range(t_len):
        p = jnp.exp(own[j] - top)
        den = den + p
        out = out + p * v_new[j:j + 1, :]
    row_o = lax.broadcasted_iota(jnp.int32, q.shape, 0)
    for s in range(n_slabs):
        p = jnp.exp(past[s] - top)
        den = den + jnp.sum(p, axis=-1, keepdims=True)
        p_bf = p.astype(BF16)
        for t in range(t_len):
            o_t = _dot(p_bf, v_refs[t * n_slabs + s][...].astype(BF16))
            out = out + jnp.where(row_o == t, o_t, 0.0)
    o_ref[0, 0] = out / den


def sample_attend(q, k_new, v_new, cache_k, cache_v, phys, *, t_len, n_slabs):
    dbsz, heads, rows, hd = q.shape
    page = cache_k.shape[1]
    kern = functools.partial(_sample_attend_kernel, n_slabs=n_slabs, t_len=t_len, heads=heads, scale=hd ** -0.5)
    n_gather = t_len * n_slabs
    new_spec = pl.BlockSpec((1, 1, rows, hd), lambda i, ph: (i // heads, i % heads, 0, 0))
    any_spec = pl.BlockSpec(memory_space=pl.ANY)
    return pl.pallas_call(
        kern,
        out_shape=jax.ShapeDtypeStruct((dbsz, heads, rows, hd), F32),
        grid_spec=pltpu.PrefetchScalarGridSpec(
            num_scalar_prefetch=1,
            grid=(dbsz * heads,),
            in_specs=[new_spec, new_spec, new_spec, any_spec, any_spec],
            out_specs=new_spec,
            scratch_shapes=[
                pltpu.VMEM((2, n_gather, page, hd), F32),
                pltpu.VMEM((2, n_gather, page, hd), F32),
                pltpu.SemaphoreType.DMA((2, 2 * n_gather)),
            ],
        ),
        compiler_params=_params("arbitrary"),
        name="sample_attend",
    )(phys, q, k_new, v_new, cache_k, cache_v)


def _row_tile(n, target):
    tm = min(n, target)
    assert n % tm == 0
    return tm


def kernel(x_prompt, x_sample, state_gdn, state_conv, cache_k, cache_v, page_table, norm_mix, norm_ffn,
           gdn_w_in, gdn_conv_w, gdn_a_log, gdn_dt_bias, gdn_norm, gdn_w_out, kv_norm, w_kv, moba_w_q,
           moba_w_o, moe_w_group, moe_b_group, moe_w_router, moe_b_router, moe_w_gate, moe_w_up,
           moe_w_down, norm_final):
    bsz, s, d = x_prompt.shape
    dbsz, t, _ = x_sample.shape
    depth = norm_mix.shape[0]
    n_a = gdn_w_in.shape[0]
    conv_w, qkv_dim = gdn_conv_w.shape[1:]
    gdn_heads, dv = gdn_a_log.shape[1], gdn_norm.shape[1]
    z_dim = gdn_heads * dv
    n_pool, page, moba_heads, hd = cache_k.shape
    attn_dim = moba_heads * hd
    n_pages = page_table.shape[1]
    ppb = MOBA_BLOCK // page
    n_full = n_pages // ppb
    n_groups = moe_w_group.shape[-1]
    n_experts = moe_w_router.shape[-1]
    assert n_pages == n_full * ppb, "a partial past block (tail pages) is not supported"
    assert n_full >= MOBA_TOPK and t >= conv_w - 1 and s >= conv_w - 1 and t <= page

    groups = [(x_prompt.reshape(bsz * s, d), bsz, s), (x_sample.reshape(dbsz * t, d), dbsz, t)]
    tiles = [_row_tile(bsz * s, ROW_TILE), _row_tile(dbsz * t, ROW_TILE)]

    def moe(l, x, tm):
        w_route = jnp.zeros((d, LANES), F32).at[:, :n_groups].set(moe_w_group[l])
        w_route = w_route.at[:, n_groups:n_groups + n_experts].set(moe_w_router[l])
        b_route = jnp.zeros((1, LANES), F32).at[0, :n_groups].set(moe_b_group[l])
        b_route = b_route.at[0, n_groups:n_groups + n_experts].set(moe_b_router[l])
        return hier_moe_residual(x, norm_ffn[l], w_route, b_route, moe_w_gate[l].astype(BF16),
                                 moe_w_up[l].astype(BF16), moe_w_down[l].astype(BF16),
                                 n_groups=n_groups, tm=tm, g_out=norm_final if l == depth - 1 else None)

    xs = [g[0] for g in groups]
    gdn_states, conv_states, kvs = [[], []], [[], []], [None, None]
    for l in range(depth):
        if l < n_a:
            in_dim = gdn_w_in.shape[2]
            in_pad = -(-in_dim // GDN_PROJ_COL_TILE) * GDN_PROJ_COL_TILE
            assert qkv_dim + z_dim + LANES <= in_pad and in_dim - qkv_dim - z_dim == 2 * gdn_heads
            w_in = jnp.pad(gdn_w_in[l], ((0, 0), (0, in_pad - in_dim))).astype(BF16)
            w_out = gdn_w_out[l].astype(BF16)
            for gi, (_, nb, tl) in enumerate(groups):
                tm = tiles[gi]
                proj = norm_matmul(xs[gi], norm_mix[l], w_in, tm=tm, tn=GDN_PROJ_COL_TILE).reshape(nb, tl, in_pad)
                tp = -(-tl // GDN_CHUNK) * GDN_CHUNK
                proj_p = jnp.pad(proj, ((0, 0), (0, tp - tl), (0, 0))) if tp != tl else proj
                if gi == 0:
                    conv0 = jnp.zeros((nb, conv_w - 1, qkv_dim), F32)
                    s0 = jnp.zeros((nb, gdn_heads, (qkv_dim // gdn_heads - dv) // 2, dv), F32)
                else:
                    conv0, s0 = state_conv[l], state_gdn[l]
                with_means = gi == 0 and l == 0
                og, s_fin, means = gdn_core(proj_p, conv0, s0, gdn_conv_w[l], gdn_a_log[l], gdn_dt_bias[l],
                                            gdn_norm[l], t_valid=tl, cache_k=cache_k if with_means else None,
                                            page_ids=page_table.reshape(-1) if with_means else None)
                if with_means:
                    bmean = means.reshape(dbsz, n_full, moba_heads, hd).transpose(0, 2, 1, 3)
                og = og[:, :tl].reshape(nb * tl, z_dim)
                gdn_states[gi].append(s_fin)
                conv_states[gi].append(proj[:, tl - (conv_w - 1):, :qkv_dim])
                xs[gi] = matmul_residual(og, w_out, xs[gi], tm=tm, tn=COL_TILE)
        else:
            j = l - n_a
            w_q = moba_w_q[j].astype(BF16)
            w_o = moba_w_o[j].astype(BF16)
            q = norm_matmul(xs[0], norm_mix[l], w_q, tm=tiles[0], tn=COL_TILE).reshape(bsz, s, attn_dim)
            k_p, v_p = kvs[0]
            o = moba_prompt(q, k_p.reshape(bsz, s, attn_dim), v_p.reshape(bsz, s, attn_dim), heads=moba_heads)
            xs[0] = matmul_residual(o.reshape(bsz * s, attn_dim), w_o, xs[0], tm=tiles[0], tn=COL_TILE)
            q = norm_matmul(xs[1], norm_mix[l], w_q, tm=tiles[1], tn=COL_TILE)
            k_s, v_s = kvs[1]
            n_sel = min(MOBA_TOPK, n_full)
            phys = sample_select(q.reshape(dbsz, t * moba_heads, hd), bmean, page_table,
                                 heads=moba_heads, n_sel=n_sel, ppb=ppb)
            phys = phys[:, :, :n_sel * ppb].reshape(-1)

            def head_major(a):
                a = a.reshape(dbsz, t, moba_heads, hd).transpose(0, 2, 1, 3)
                return jnp.pad(a, ((0, 0), (0, 0), (0, -t % 8), (0, 0)))

            o = sample_attend(head_major(q), head_major(k_s), head_major(v_s), cache_k, cache_v, phys,
                              t_len=t, n_slabs=n_sel * ppb)
            o = o[:, :, :t].transpose(0, 2, 1, 3).reshape(dbsz * t, attn_dim)
            xs[1] = matmul_residual(o, w_o, xs[1], tm=tiles[1], tn=COL_TILE)
        for gi in range(2):
            xs[gi] = moe(l, xs[gi], _row_tile(xs[gi].shape[0], MOE_TOKEN_TILE))
        if l == n_a - 1:
            w_kv_bf = w_kv.astype(BF16)
            for gi, (_, nb, tl) in enumerate(groups):
                kv = norm_matmul(xs[gi], kv_norm, w_kv_bf, tm=tiles[gi], tn=COL_TILE)
                kvs[gi] = (kv[:, :attn_dim].reshape(nb, tl, moba_heads, hd),
                           kv[:, attn_dim:].reshape(nb, tl, moba_heads, hd))
    y_p = xs[0].reshape(bsz, s, d)
    y_s = xs[1].reshape(dbsz, t, d)
    return (y_p, y_s, jnp.stack(gdn_states[0]), jnp.stack(conv_states[0]), kvs[0][0], kvs[0][1],
            jnp.stack(gdn_states[1]), jnp.stack(conv_states[1]), kvs[1][0], kvs[1][1])
```
